```python
import math
import jax, jax.numpy as jnp
from jax import lax
import numpy as np

D_MODEL = 1024
BATCH = 4
SEQ = 8192
DEPTH = 1

N_META = 16
CONV_CH = D_MODEL
CONV_WIDTH = 31
RET_HEADS = 4
RET_DK = 256
RET_DV = 512
RET_CHUNK = 128
ROPE_BASE = 10000.0
COLS_CONV = 2 * CONV_CH
COLS_Q = RET_HEADS * RET_DK
COLS_K = RET_HEADS * RET_DK
COLS_V = RET_HEADS * RET_DV
COLS_G = RET_HEADS * RET_DV
COLS_GATE = 2 * D_MODEL
PROJ_COLS = COLS_CONV + COLS_Q + COLS_K + COLS_V + COLS_G + COLS_GATE
N_EXPERTS = 32
TOP_K = 4
D_EXPERT = D_MODEL
SWIGLU_LIMIT = 7.0
SWIGLU_ALPHA = 1.702
EXPERT_BLOCK = 128
EPS = 1e-5

kernel_name = "hybrid_conv_retention_moe_block"


def rmsnorm(x, w):
    x32 = x.astype(jnp.float32)
    y = x32 * lax.rsqrt(jnp.mean(x32 * x32, axis=-1, keepdims=True) + EPS)
    return (y * w.astype(jnp.float32)).astype(x.dtype)


def layernorm(x, w, b):
    x32 = x.astype(jnp.float32)
    mu = jnp.mean(x32, axis=-1, keepdims=True)
    xc = x32 - mu
    y = xc * lax.rsqrt(jnp.mean(xc * xc, axis=-1, keepdims=True) + EPS)
    return (y * w.astype(jnp.float32) + b.astype(jnp.float32)).astype(x.dtype)


def rotary(x, positions):
    d = x.shape[-1]
    inv_freq = ROPE_BASE ** (-jnp.arange(0, d, 2, dtype=jnp.float32) / d)
    ang = positions.astype(jnp.float32)[:, None] * inv_freq[None, :]
    cos = jnp.cos(ang)[None, :, None, :]
    sin = jnp.sin(ang)[None, :, None, :]
    x32 = x.astype(jnp.float32)
    x1, x2 = x32[..., : d // 2], x32[..., d // 2:]
    return jnp.concatenate([x1 * cos - x2 * sin, x2 * cos + x1 * sin], axis=-1)


def causal_depthwise_conv(u, w, b):
    out = lax.conv_general_dilated(
        u, w[:, None, :].astype(u.dtype), window_strides=(1,),
        padding=[(CONV_WIDTH - 1, 0)],
        dimension_numbers=("NWC", "WIO", "NWC"),
        feature_group_count=u.shape[-1])
    return out + b


def chunkwise_retention(q, k, v):
    Bsz, L, H, _ = q.shape
    C = RET_CHUNK
    pad = C - N_META
    padf = lambda t: jnp.pad(t.astype(jnp.float32), ((0, 0), (pad, 0), (0, 0), (0, 0)))
    q, k, v = padf(q), padf(k), padf(v)
    Lp = L + pad
    nC = Lp // C
    q = q.reshape(Bsz, nC, C, H, RET_DK)
    k = k.reshape(Bsz, nC, C, H, RET_DK)
    v = v.reshape(Bsz, nC, C, H, RET_DV)

    log_gamma = jnp.log1p(-jnp.exp2(-5.0 - jnp.arange(H, dtype=jnp.float32)))
    idx = jnp.arange(C, dtype=jnp.float32)
    diff = idx[:, None] - idx[None, :]
    decay_in = jnp.where(diff[None] >= 0,
                         jnp.exp(log_gamma[:, None, None] * jnp.maximum(diff, 0.0)[None]), 0.0)
    zeta = jnp.exp(log_gamma[None, :] * (C - 1 - idx)[:, None])
    xi = jnp.exp(log_gamma[None, :] * (idx + 1.0)[:, None])
    chunk_decay = jnp.exp(log_gamma * C)

    scores = jnp.einsum("bnihd,bnjhd->bnhij", q, k) * decay_in[None, None]
    inner = jnp.einsum("bnhij,bnjhe->bnihe", scores, v)

    kz = k * zeta[None, None, :, :, None]
    def step(R, inp):
        qc, kc, vc = inp
        cross = jnp.einsum("bihd,bhde->bihe", qc, R)
        R = R * chunk_decay[None, :, None, None] + jnp.einsum("bjhd,bjhe->bhde", kc, vc)
        return R, cross
    R0 = jnp.zeros((Bsz, H, RET_DK, RET_DV), jnp.float32)
    _, cross = lax.scan(step, R0, (q.transpose(1, 0, 2, 3, 4),
                                   kz.transpose(1, 0, 2, 3, 4),
                                   v.transpose(1, 0, 2, 3, 4)))
    cross = cross.transpose(1, 0, 2, 3, 4) * xi[None, None, :, :, None]
    y = (inner + cross).reshape(Bsz, Lp, H, RET_DV)
    return y[:, pad:]


def token_mixer(h, w_in, dw_w, dw_b, ln_w, ln_b, w_conv_out, b_conv_out, gn_w, w_ret_out, w_mix_out):
    Bsz, L, _ = h.shape
    z = h @ w_in
    splits = np.cumsum([COLS_CONV, COLS_Q, COLS_K, COLS_V, COLS_G]).tolist()
    z_conv, z_q, z_k, z_v, z_g, z_gate = jnp.split(z, splits, axis=-1)

    a, b = jnp.split(z_conv, 2, axis=-1)
    u = a * jax.nn.sigmoid(b)
    u = causal_depthwise_conv(u, dw_w, dw_b)
    u = jax.nn.silu(layernorm(u, ln_w, ln_b))
    y_conv = u @ w_conv_out + b_conv_out

    pos = jnp.arange(L)
    q = rotary(z_q.reshape(Bsz, L, RET_HEADS, RET_DK), pos) * (RET_DK ** -0.5)
    k = rotary(z_k.reshape(Bsz, L, RET_HEADS, RET_DK), pos)
    v = z_v.reshape(Bsz, L, RET_HEADS, RET_DV)
    y = chunkwise_retention(q, k, v)
    mu = jnp.mean(y, axis=-1, keepdims=True)
    yc = y - mu
    y = yc * lax.rsqrt(jnp.mean(yc * yc, axis=-1, keepdims=True) + EPS)
    y = y.reshape(Bsz, L, RET_HEADS * RET_DV) * gn_w.astype(jnp.float32)
    y_ret = (jax.nn.silu(z_g.astype(jnp.float32)) * y).astype(h.dtype) @ w_ret_out

    g_a, g_b = jnp.split(z_gate, 2, axis=-1)
    merged = jax.nn.sigmoid(g_a) * y_conv + jax.nn.sigmoid(g_b) * y_ret
    return merged @ w_mix_out


def moe_ffn(xn, w_router, b_router, w_gate_up, b_gate_up, w_down, b_down):
    Bsz, L, D = xn.shape
    T = Bsz * L
    xt = xn.reshape(T, D)
    logits = (xt @ w_router + b_router).astype(jnp.float32)
    top_v, top_i = lax.top_k(logits, TOP_K)
    gates = jax.nn.softmax(top_v, axis=-1)

    A = T * TOP_K
    e_flat = top_i.reshape(A)
    tok_flat = jnp.arange(A, dtype=jnp.int32) // TOP_K
    w_flat = gates.reshape(A)
    order = jnp.argsort(e_flat)
    e_sorted = e_flat[order]
    counts = jnp.bincount(e_flat, length=N_EXPERTS)
    starts = jnp.cumsum(counts) - counts
    pcounts = (counts + EXPERT_BLOCK - 1) // EXPERT_BLOCK * EXPERT_BLOCK
    pends = jnp.cumsum(pcounts)
    pstarts = pends - pcounts
    dest = pstarts[e_sorted] + (jnp.arange(A) - starts[e_sorted])
    n_blocks = -(-A // EXPERT_BLOCK) + N_EXPERTS
    P = n_blocks * EXPERT_BLOCK
    row_tok = jnp.full((P,), T, jnp.int32).at[dest].set(tok_flat[order])
    row_w = jnp.zeros((P,), jnp.float32).at[dest].set(w_flat[order])
    block_e = jnp.minimum(jnp.searchsorted(pends, jnp.arange(n_blocks) * EXPERT_BLOCK, side="right"),
                          N_EXPERTS - 1)
    x_pad = jnp.concatenate([xt, jnp.zeros((1, D), xt.dtype)], axis=0)
    xs = x_pad[row_tok].reshape(n_blocks, EXPERT_BLOCK, D)

    def expert_block(args):
        xb, e = args
        hgu = xb @ w_gate_up[e] + b_gate_up[e]
        h_gate, h_up = jnp.split(hgu, 2, axis=-1)
        h_gate = jnp.minimum(h_gate, SWIGLU_LIMIT)
        h_up = jnp.clip(h_up, -SWIGLU_LIMIT, SWIGLU_LIMIT)
        act = h_gate * jax.nn.sigmoid(SWIGLU_ALPHA * h_gate) * (h_up + 1.0)
        return act @ w_down[e] + b_down[e]

    ys = lax.map(expert_block, (xs, block_e)).reshape(P, D)
    out = jax.ops.segment_sum(ys * row_w[:, None], row_tok, num_segments=T + 1)[:T]
    return out.reshape(Bsz, L, D).astype(xn.dtype)


def setup_inputs(seed: int = 0) -> dict:
    key = jax.random.key(seed)
    ks = jax.random.split(key, 24)
    f32 = jnp.float32
    nrm = lambda k, shape, scale: jax.random.normal(k, shape, f32) * scale
    gain = lambda k, shape: 1.0 + 0.02 * jax.random.normal(k, shape, f32)
    return {
        "x": jax.random.normal(ks[0], (BATCH, SEQ, D_MODEL), f32),
        "meta_tokens": nrm(ks[1], (N_META, D_MODEL), 1.0),
        "norm_mix_w": gain(ks[2], (DEPTH, D_MODEL)),
        "w_in": nrm(ks[3], (DEPTH, D_MODEL, PROJ_COLS), D_MODEL ** -0.5),
        "conv_dw_w": nrm(ks[4], (DEPTH, CONV_WIDTH, CONV_CH), CONV_WIDTH ** -0.5),
        "conv_dw_b": nrm(ks[5], (DEPTH, CONV_CH), 0.02),
        "conv_ln_w": gain(ks[6], (DEPTH, CONV_CH)),
        "conv_ln_b": nrm(ks[7], (DEPTH, CONV_CH), 0.02),
        "w_conv_out": nrm(ks[8], (DEPTH, CONV_CH, D_MODEL), CONV_CH ** -0.5),
        "b_conv_out": nrm(ks[9], (DEPTH, D_MODEL), 0.02),
        "ret_gn_w": gain(ks[10], (DEPTH, RET_HEADS * RET_DV)),
        "w_ret_out": nrm(ks[11], (DEPTH, RET_HEADS * RET_DV, D_MODEL), (RET_HEADS * RET_DV) ** -0.5),
        "w_mix_out": nrm(ks[12], (DEPTH, D_MODEL, D_MODEL), D_MODEL ** -0.5),
        "norm_ffn_w": gain(ks[13], (DEPTH, D_MODEL)),
        "w_router": nrm(ks[14], (DEPTH, D_MODEL, N_EXPERTS), D_MODEL ** -0.5),
        "b_router": nrm(ks[15], (DEPTH, N_EXPERTS), 0.01),
        "w_gate_up": nrm(ks[16], (DEPTH, N_EXPERTS, D_MODEL, 2 * D_EXPERT), D_MODEL ** -0.5),
        "b_gate_up": nrm(ks[17], (DEPTH, N_EXPERTS, 2 * D_EXPERT), 0.02),
        "w_down": nrm(ks[18], (DEPTH, N_EXPERTS, D_EXPERT, D_MODEL), D_EXPERT ** -0.5),
        "b_down": nrm(ks[19], (DEPTH, N_EXPERTS, D_MODEL), 0.02),
        "norm_final_w": gain(ks[20], (D_MODEL,)),
    }


def reference(x, meta_tokens, norm_mix_w, w_in, conv_dw_w, conv_dw_b, conv_ln_w, conv_ln_b,
              w_conv_out, b_conv_out, ret_gn_w, w_ret_out, w_mix_out, norm_ffn_w,
              w_router, b_router, w_gate_up, b_gate_up, w_down, b_down, norm_final_w):
    Bsz = x.shape[0]
    meta = jnp.broadcast_to(meta_tokens[None].astype(x.dtype), (Bsz, N_META, x.shape[-1]))
    h = jnp.concatenate([meta, x], axis=1)
    for layer in range(DEPTH):
        h = h + token_mixer(rmsnorm(h, norm_mix_w[layer]), w_in[layer], conv_dw_w[layer],
                            conv_dw_b[layer], conv_ln_w[layer], conv_ln_b[layer],
                            w_conv_out[layer], b_conv_out[layer], ret_gn_w[layer],
                            w_ret_out[layer], w_mix_out[layer])
        h = h + moe_ffn(rmsnorm(h, norm_ffn_w[layer]), w_router[layer], b_router[layer],
                        w_gate_up[layer], b_gate_up[layer], w_down[layer], b_down[layer])
    out = rmsnorm(h, norm_final_w)
    return out[:, N_META:]
```

```python
import functools

import jax
import jax.numpy as jnp
from jax import lax
from jax.experimental import pallas as pl
from jax.experimental.pallas import tpu as pltpu

N_META = 16
CONV_WIDTH = 31
RET_HEADS = 4
RET_DK = 256
RET_DV = 512
RET_CHUNK = 128
ROPE_BASE = 10000.0
TOP_K = 4
SWIGLU_LIMIT = 7.0
SWIGLU_ALPHA = 1.702
EPS = 1e-5

LANES = 128
CONV_HALO = 32
VMEM_LIMIT = 56 * 1024 * 1024

F32 = jnp.float32
BF16 = jnp.bfloat16
I32 = jnp.int32

ZB_V, ZB_SG, ZB_GATE, ZB_U, ZB_Q, ZB_K = 0, 2, 4, 6, 7, 8
Z_COLS = 9 * 1024


def _params(n_grid_axes):
    return pltpu.CompilerParams(dimension_semantics=("arbitrary",) * n_grid_axes,
                                vmem_limit_bytes=VMEM_LIMIT)


def _inproj_body(x_ref, nw_ref, w_ref, cos_ref, sin_ref, o_ref, hn_ref, a_ref):
    j = pl.program_id(1)

    @pl.when(j == 0)
    def _norm():
        x = x_ref[...]
        ms = jnp.mean(x * x, axis=-1, keepdims=True)
        hn_ref[...] = (x * lax.rsqrt(ms + EPS) * nw_ref[...]).astype(BF16)

    acc = jnp.dot(hn_ref[...], w_ref[...], preferred_element_type=F32)

    @pl.when(j == 0)
    def _glu_a():
        a_ref[...] = acc

    @pl.when(j == 1)
    def _glu():
        o_ref[...] = (a_ref[...] * jax.nn.sigmoid(acc)).astype(BF16)

    @pl.when((j == 2) | (j == 3))
    def _rotary():
        scale = jnp.where(j == 2, RET_DK ** -0.5, 1.0).astype(F32)
        cos = cos_ref[...] * scale
        sin = sin_ref[...] * scale
        half = RET_DK // 2
        for h in range(RET_HEADS):
            x1 = acc[:, h * RET_DK:h * RET_DK + half]
            x2 = acc[:, h * RET_DK + half:(h + 1) * RET_DK]
            o_ref[:, h * RET_DK:h * RET_DK + half] = (x1 * cos - x2 * sin).astype(BF16)
            o_ref[:, h * RET_DK + half:(h + 1) * RET_DK] = (x2 * cos + x1 * sin).astype(BF16)

    @pl.when((j == 4) | (j == 5))
    def _value():
        o_ref[...] = acc.astype(BF16)

    @pl.when((j == 6) | (j == 7))
    def _silu():
        o_ref[...] = (acc * jax.nn.sigmoid(acc)).astype(BF16)

    @pl.when(j >= 8)
    def _gate():
        o_ref[...] = jax.nn.sigmoid(acc).astype(BF16)


def _inproj(x2d, norm_w, w_in_bf, cos, sin, tm):
    rows, d = x2d.shape
    period_tiles = cos.shape[0] // tm

    def out_map(i, j):
        return i, jnp.where(j < 2, ZB_U, jnp.where(j < 4, j + (ZB_Q - 2), j - 4))

    return pl.pallas_call(
        _inproj_body,
        out_shape=jax.ShapeDtypeStruct((rows, Z_COLS), BF16),
        grid=(rows // tm, 10),
        in_specs=[
            pl.BlockSpec((tm, d), lambda i, j: (i, 0)),
            pl.BlockSpec((1, d), lambda i, j: (0, 0)),
            pl.BlockSpec((d, 1024), lambda i, j: (0, j)),
            pl.BlockSpec((tm, LANES), lambda i, j: (i % period_tiles, 0)),
            pl.BlockSpec((tm, LANES), lambda i, j: (i % period_tiles, 0)),
        ],
        out_specs=pl.BlockSpec((tm, 1024), out_map),
        scratch_shapes=[pltpu.VMEM((tm, d), BF16), pltpu.VMEM((tm, 1024), F32)],
        compiler_params=_params(2),
        name="inproj",
    )(x2d, norm_w, w_in_bf, cos, sin)


def _dot_t0(a, b):
    return lax.dot_general(a, b, (((0,), (0,)), ((), ())), preferred_element_type=F32)


def _dot_t1(a, b):
    return lax.dot_general(a, b, (((1,), (1,)), ((), ())), preferred_element_type=F32)


def _retention_body(cd_ref, q_ref, k_ref, v_ref, sg_ref, zm_ref, dmat_ref, zeta_ref, xi_ref, gnw_ref,
                    o_ref, state_ref, *, n_chunks):
    t = pl.program_id(1)
    C, DK, DV = RET_CHUNK, RET_DK, RET_DV

    def state_update(prev, h, k_h, v_h):
        kz = (k_h.astype(F32) * zeta_ref[h]).astype(BF16)
        return prev * cd_ref[h] + _dot_t0(kz, v_h)

    @pl.when(t == 0)
    def _seed_from_meta_chunk():
        for h in range(RET_HEADS):
            k_h = zm_ref[:, ZB_K * 1024 + h * DK:ZB_K * 1024 + (h + 1) * DK]
            v_h = zm_ref[:, ZB_V * 1024 + h * DV:ZB_V * 1024 + (h + 1) * DV]
            state_ref[h] = state_update(jnp.zeros((DK, DV), F32), h, k_h, v_h)

    def chunk(c, carry):
        rows = pl.ds(pl.multiple_of(c * C, C), C)
        for h in range(RET_HEADS):
            q_h = q_ref[rows, h * DK:(h + 1) * DK]
            k_h = k_ref[rows, h * DK:(h + 1) * DK]
            v_h = v_ref[rows, h * DV:(h + 1) * DV]
            scores = _dot_t1(q_h, k_h) * dmat_ref[h]
            inner = jnp.dot(scores.astype(BF16), v_h, preferred_element_type=F32)
            state = state_ref[h]
            cross = jnp.dot(q_h, state.astype(BF16), preferred_element_type=F32) * xi_ref[h]
            state_ref[h] = state_update(state, h, k_h, v_h)
            y = inner + cross
            mu = jnp.mean(y, axis=-1, keepdims=True)
            yc = y - mu
            yn = yc * lax.rsqrt(jnp.mean(yc * yc, axis=-1, keepdims=True) + EPS)
            gated = yn * gnw_ref[:, h * DV:(h + 1) * DV] * sg_ref[rows, h * DV:(h + 1) * DV].astype(F32)
            o_ref[rows, h * DV:(h + 1) * DV] = gated.astype(BF16)
        return carry

    lax.fori_loop(0, n_chunks, chunk, 0)


def _retention(zz, zm, tables, gn_w, batch, seq, tt):
    chunk_decay, dmat, zeta, xi = tables
    tiles = seq // tt
    row = lambda b, t: b * tiles + t
    hv = RET_HEADS * RET_DV
    full = lambda shape: pl.BlockSpec(shape, lambda b, t: (0,) * len(shape))
    return pl.pallas_call(
        functools.partial(_retention_body, n_chunks=tt // RET_CHUNK),
        out_shape=jax.ShapeDtypeStruct((batch * seq, hv), BF16),
        grid=(batch, tiles),
        in_specs=[
            pl.BlockSpec(memory_space=pltpu.SMEM),
            pl.BlockSpec((tt, 1024), lambda b, t: (row(b, t), ZB_Q)),
            pl.BlockSpec((tt, 1024), lambda b, t: (row(b, t), ZB_K)),
            pl.BlockSpec((tt, hv), lambda b, t: (row(b, t), ZB_V // 2)),
            pl.BlockSpec((tt, hv), lambda b, t: (row(b, t), ZB_SG // 2)),
            full((RET_CHUNK, Z_COLS)),
            full((RET_HEADS, RET_CHUNK, RET_CHUNK)),
            full((RET_HEADS, RET_CHUNK, 1)),
            full((RET_HEADS, RET_CHUNK, 1)),
            full((1, hv)),
        ],
        out_specs=pl.BlockSpec((tt, hv), lambda b, t: (row(b, t), 0)),
        scratch_shapes=[pltpu.VMEM((RET_HEADS, RET_DK, RET_DV), F32)],
        compiler_params=_params(2),
        name="retention",
    )(chunk_decay, zz, zz, zz, zz, zm, dmat, zeta, xi, gn_w)


def _tail_body(x_ref, u_ref, gates_ref, yg_ref, zm_ref, dww_ref, dwb_ref, lnw_ref, lnb_ref,
               wco_ref, bco_ref, wro_ref, wmo_ref, nfw_ref, wrt_ref, br_ref, tri_ref,
               h2_ref, xn_ref, topi_ref, gate_ref, rank_ref, cnt_ref,
               uext_ref, conv_ref, count_ref, *, tt, n_experts):
    b = pl.program_id(0)
    t = pl.program_id(1)
    C = RET_CHUNK
    d = x_ref.shape[1]
    slab = 256

    @pl.when((b == 0) & (t == 0))
    def _zero_counts():
        count_ref[...] = jnp.zeros_like(count_ref)

    @pl.when(t == 0)
    def _seed_conv_history():
        uext_ref[0:CONV_HALO, :] = zm_ref[C - CONV_HALO:C, ZB_U * 1024:(ZB_U + 1) * 1024].astype(F32)

    uext_ref[CONV_HALO:CONV_HALO + tt, :] = u_ref[...].astype(F32)

    first = CONV_HALO - (CONV_WIDTH - 1)
    for c in range(tt // C):
        for s in range(d // slab):
            cols = slice(s * slab, (s + 1) * slab)
            acc = jnp.zeros((C, slab), F32)
            for k in range(CONV_WIDTH):
                r0 = c * C + first + k
                acc = acc + dww_ref[k:k + 1, cols] * uext_ref[r0:r0 + C, cols]
            conv_ref[c * C:(c + 1) * C, cols] = acc + dwb_ref[:, cols]
    uext_ref[0:CONV_HALO, :] = uext_ref[tt:tt + CONV_HALO, :]

    cv = conv_ref[...]
    mu = jnp.mean(cv, axis=-1, keepdims=True)
    cc = cv - mu
    ln = cc * lax.rsqrt(jnp.mean(cc * cc, axis=-1, keepdims=True) + EPS) * lnw_ref[...] + lnb_ref[...]
    act = (ln * jax.nn.sigmoid(ln)).astype(BF16)
    y_conv = jnp.dot(act, wco_ref[...], preferred_element_type=F32) + bco_ref[...]
    y_ret = jnp.dot(yg_ref[...], wro_ref[...], preferred_element_type=F32)
    merged = (gates_ref[:, :d].astype(F32) * y_conv + gates_ref[:, d:].astype(F32) * y_ret).astype(BF16)
    h2 = x_ref[...] + jnp.dot(merged, wmo_ref[...], preferred_element_type=F32)
    h2_ref[...] = h2

    xn = h2 * lax.rsqrt(jnp.mean(h2 * h2, axis=-1, keepdims=True) + EPS) * nfw_ref[...]
    xn_ref[...] = xn

    logits = lax.dot_general(wrt_ref[...], xn, (((1,), (1,)), ((), ())),
                             precision=lax.Precision.HIGHEST, preferred_element_type=F32) + br_ref[...]
    eidx = lax.broadcasted_iota(I32, (n_experts, tt), 0)
    selected = jnp.zeros((n_experts, tt), F32)
    vals, idxs = [], []
    for _ in range(TOP_K):
        m = jnp.max(logits, axis=0, keepdims=True)
        idx = jnp.min(jnp.where(logits == m, eidx, n_experts), axis=0, keepdims=True)
        hit = eidx == idx
        vals.append(m)
        idxs.append(idx)
        selected = jnp.where(hit, 1.0, selected)
        logits = jnp.where(hit, -jnp.inf, logits)
    exps = [jnp.exp(v - vals[0]) for v in vals]
    denom = exps[0] + exps[1] + exps[2] + exps[3]

    before = jnp.dot(selected.astype(BF16), tri_ref[...], preferred_element_type=F32) + count_ref[:, 0:1]
    ranks = [jnp.sum(jnp.where(eidx == idx, before, 0.0), axis=0, keepdims=True).astype(I32) for idx in idxs]
    count_ref[...] = count_ref[...] + jnp.sum(selected, axis=1, keepdims=True)

    zi = jnp.zeros((8 - TOP_K, tt), I32)
    topi_ref[...] = jnp.concatenate(idxs + [zi], axis=0)
    rank_ref[...] = jnp.concatenate(ranks + [zi], axis=0)
    gate_ref[...] = jnp.concatenate([e / denom for e in exps] + [zi.astype(F32)], axis=0)
    cnt_ref[...] = count_ref[...]


def _tail(x2d, zz, yg, zm, p, batch, seq, tt, n_experts):
    d = x2d.shape[1]
    tiles = seq // tt
    rows = batch * seq
    row = lambda b, t: b * tiles + t
    full = lambda shape: pl.BlockSpec(shape, lambda b, t: (0,) * len(shape))
    tri = (jnp.arange(tt)[:, None] < jnp.arange(tt)[None, :]).astype(BF16)
    lane_rows = pl.BlockSpec((8, tt), lambda b, t: (0, row(b, t)))
    return pl.pallas_call(
        functools.partial(_tail_body, tt=tt, n_experts=n_experts),
        out_shape=(
            jax.ShapeDtypeStruct((rows, d), F32),
            jax.ShapeDtypeStruct((rows, d), F32),
            jax.ShapeDtypeStruct((8, rows), I32),
            jax.ShapeDtypeStruct((8, rows), F32),
            jax.ShapeDtypeStruct((8, rows), I32),
            jax.ShapeDtypeStruct((n_experts, LANES), F32),
        ),
        grid=(batch, tiles),
        in_specs=[
            pl.BlockSpec((tt, d), lambda b, t: (row(b, t), 0)),
            pl.BlockSpec((tt, 1024), lambda b, t: (row(b, t), ZB_U)),
            pl.BlockSpec((tt, 2048), lambda b, t: (row(b, t), ZB_GATE // 2)),
            pl.BlockSpec((tt, yg.shape[1]), lambda b, t: (row(b, t), 0)),
            full((RET_CHUNK, Z_COLS)),
            full((CONV_WIDTH, d)), full((1, d)), full((1, d)), full((1, d)),
            full((d, d)), full((1, d)),
            full((yg.shape[1], d)),
            full((d, d)),
            full((1, d)),
            full((n_experts, d)), full((n_experts, 1)),
            full((tt, tt)),
        ],
        out_specs=(
            pl.BlockSpec((tt, d), lambda b, t: (row(b, t), 0)),
            pl.BlockSpec((tt, d), lambda b, t: (row(b, t), 0)),
            lane_rows, lane_rows, lane_rows,
            full((n_experts, LANES)),
        ),
        scratch_shapes=[
            pltpu.VMEM((CONV_HALO + tt, d), F32),
            pltpu.VMEM((tt, d), F32),
            pltpu.VMEM((n_experts, LANES), F32),
        ],
        compiler_params=_params(2),
        name="mixer_tail",
    )(x2d, zz, zz, yg, zm, p["dw_w"], p["dw_b"], p["ln_w"], p["ln_b"], p["w_conv_out"], p["b_conv_out"],
      p["w_ret_out"], p["w_mix_out"], p["norm_ffn_w"], p["w_router_t"], p["b_router"], tri)


def _row_copy(src_hbm, src_row, dst_hbm, dst_row, sem):
    return pltpu.make_async_copy(src_hbm.at[pl.ds(src_row, 1)], dst_hbm.at[pl.ds(dst_row, 1)], sem)


def _dispatch_body(pend_ref, pcnt_ref, dest_ref, xn_hbm, xs_hbm, zero_ref, sem, zsem, *, td, tm, n_experts):
    i = pl.program_id(0)

    def zero_copy(e):
        start = pl.multiple_of(pend_ref[e] - tm, tm)
        return pltpu.make_async_copy(zero_ref, xs_hbm.at[pl.ds(start, tm)], zsem)

    @pl.when(i == 0)
    def _zero_last_tile_of_every_expert():
        zero_ref[...] = jnp.zeros_like(zero_ref)
        for e in range(n_experts):
            pl.when(pcnt_ref[e] > 0)(lambda e=e: zero_copy(e).start())
        for e in range(n_experts):
            pl.when(pcnt_ref[e] > 0)(lambda e=e: zero_copy(e).wait())

    def issue(r, carry):
        for k in range(TOP_K):
            _row_copy(xn_hbm, i * td + r, xs_hbm, dest_ref[k * td + r], sem).start()
        return carry

    lax.fori_loop(0, td, issue, 0, unroll=8)

    def drain(r, carry):
        for k in range(TOP_K):
            _row_copy(xn_hbm, 0, xs_hbm, 0, sem).wait()
        return carry

    lax.fori_loop(0, td, drain, 0, unroll=8)


def _dispatch(pends, pcounts, dest_tiles, xn, n_rows_out, td, tm, n_experts):
    rows, w = xn.shape
    return pl.pallas_call(
        functools.partial(_dispatch_body, td=td, tm=tm, n_experts=n_experts),
        out_shape=jax.ShapeDtypeStruct((n_rows_out, w), F32),
        grid_spec=pltpu.PrefetchScalarGridSpec(
            num_scalar_prefetch=2,
            grid=(rows // td,),
            in_specs=[
                pl.BlockSpec((TOP_K * td,), lambda i, *_: (i,), memory_space=pltpu.SMEM),
                pl.BlockSpec(memory_space=pl.ANY),
            ],
            out_specs=pl.BlockSpec(memory_space=pl.ANY),
            scratch_shapes=[pltpu.VMEM((tm, w), F32), pltpu.SemaphoreType.DMA, pltpu.SemaphoreType.DMA],
        ),
        compiler_params=_params(1),
        name="moe_dispatch",
    )(pends, pcounts, dest_tiles, xn)


def _experts_body(tile_e_ref, n_used_ref, xs_ref, wgu_ref, bgu_ref, wd_ref, bd_ref, ys_ref):
    @pl.when(pl.program_id(0) < n_used_ref[0])
    def _():
        x = xs_ref[...].astype(BF16)
        hgu = jnp.dot(x, wgu_ref[0], preferred_element_type=F32) + bgu_ref[0]
        de = hgu.shape[1] // 2
        h_gate = jnp.minimum(hgu[:, :de], SWIGLU_LIMIT)
        h_up = jnp.clip(hgu[:, de:], -SWIGLU_LIMIT, SWIGLU_LIMIT)
        act = h_gate * jax.nn.sigmoid(SWIGLU_ALPHA * h_gate) * (h_up + 1.0)
        y = jnp.dot(act.astype(BF16), wd_ref[0], preferred_element_type=F32) + bd_ref[0]
        ys_ref[...] = y


def _experts(tile_e, n_used, xs, wgu, bgu, wd, bd, tm):
    n_rows, w = xs.shape
    e, d, de2 = wgu.shape
    tile = lambda p, te, nu: jnp.minimum(p, nu[0] - 1)
    expert = lambda p, te, nu: te[tile(p, te, nu)]
    return pl.pallas_call(
        _experts_body,
        out_shape=jax.ShapeDtypeStruct((n_rows, w), F32),
        grid_spec=pltpu.PrefetchScalarGridSpec(
            num_scalar_prefetch=2,
            grid=(n_rows // tm,),
            in_specs=[
                pl.BlockSpec((tm, w), lambda p, te, nu: (tile(p, te, nu), 0)),
                pl.BlockSpec((1, d, de2), lambda p, te, nu: (expert(p, te, nu), 0, 0)),
                pl.BlockSpec((1, 1, de2), lambda p, te, nu: (expert(p, te, nu), 0, 0)),
                pl.BlockSpec((1, de2 // 2, d), lambda p, te, nu: (expert(p, te, nu), 0, 0)),
                pl.BlockSpec((1, 1, d), lambda p, te, nu: (expert(p, te, nu), 0, 0)),
            ],
            out_specs=pl.BlockSpec((tm, w), lambda p, te, nu: (tile(p, te, nu), 0)),
        ),
        compiler_params=_params(1),
        name="moe_experts",
    )(tile_e, n_used, xs, wgu, bgu, wd, bd)


def _combine_body(dest_ref, ys_hbm, h2_ref, gate_ref, nw_ref, o_ref, buf_ref, sem, *, tc):
    def issue(r, carry):
        for k in range(TOP_K):
            pltpu.make_async_copy(ys_hbm.at[pl.ds(dest_ref[k * tc + r], 1)],
                                  buf_ref.at[k, pl.ds(r, 1)], sem).start()
        return carry

    lax.fori_loop(0, tc, issue, 0, unroll=8)

    g = gate_ref[...]
    g_cols = jnp.concatenate([g, jnp.zeros((LANES - g.shape[0], tc), F32)], axis=0).T

    def drain(r, carry):
        for k in range(TOP_K):
            pltpu.make_async_copy(ys_hbm.at[pl.ds(0, 1)], buf_ref.at[k, pl.ds(0, 1)], sem).wait()
        return carry

    lax.fori_loop(0, tc, drain, 0, unroll=8)

    h = h2_ref[...]
    for k in range(TOP_K):
        h = h + g_cols[:, k:k + 1] * buf_ref[k]
    o_ref[...] = h * lax.rsqrt(jnp.mean(h * h, axis=-1, keepdims=True) + EPS) * nw_ref[...]


def _combine(dest_tiles, ys, h2, gates, norm_w, tc):
    rows, d = h2.shape
    return pl.pallas_call(
        functools.partial(_combine_body, tc=tc),
        out_shape=jax.ShapeDtypeStruct((rows, d), F32),
        grid=(rows // tc,),
        in_specs=[
            pl.BlockSpec((TOP_K * tc,), lambda i: (i,), memory_space=pltpu.SMEM),
            pl.BlockSpec(memory_space=pl.ANY),
            pl.BlockSpec((tc, d), lambda i: (i, 0)),
            pl.BlockSpec((8, tc), lambda i: (0, i)),
            pl.BlockSpec((1, d), lambda i: (0, 0)),
        ],
        out_specs=pl.BlockSpec((tc, d), lambda i: (i, 0)),
        scratch_shapes=[pltpu.VMEM((TOP_K, tc, d), F32), pltpu.SemaphoreType.DMA],
        compiler_params=_params(1),
        name="moe_combine",
    )(dest_tiles, ys, h2, gates, norm_w)


def _rope_tables(positions):
    inv_freq = ROPE_BASE ** (-jnp.arange(0, RET_DK, 2, dtype=F32) / RET_DK)
    ang = positions.astype(F32)[:, None] * inv_freq[None, :]
    return jnp.cos(ang), jnp.sin(ang)


def _retention_tables():
    c = RET_CHUNK
    log_gamma = jnp.log1p(-jnp.exp2(-5.0 - jnp.arange(RET_HEADS, dtype=F32)))
    idx = jnp.arange(c, dtype=F32)
    diff = idx[:, None] - idx[None, :]
    dmat = jnp.where(diff[None] >= 0, jnp.exp(log_gamma[:, None, None] * jnp.maximum(diff, 0.0)[None]), 0.0)
    zeta = jnp.exp(log_gamma[:, None] * (c - 1 - idx)[None, :])[:, :, None]
    xi = jnp.exp(log_gamma[:, None] * (idx + 1.0)[None, :])[:, :, None]
    chunk_decay = jnp.exp(log_gamma * c)
    return chunk_decay, dmat, zeta, xi


def _largest_tile(n, cap):
    t = cap
    while n % t:
        t //= 2
    return t


def _tile_major(a, tile):
    k, t = a.shape
    return a.reshape(k, t // tile, tile).transpose(1, 0, 2).reshape(-1)


def kernel(x, meta_tokens, norm_mix_w, w_in, conv_dw_w, conv_dw_b, conv_ln_w, conv_ln_b, w_conv_out,
           b_conv_out, ret_gn_w, w_ret_out, w_mix_out, norm_ffn_w, w_router, b_router, w_gate_up,
           b_gate_up, w_down, b_down, norm_final_w):
    batch, seq, d = x.shape
    depth = w_in.shape[0]
    n_experts = w_router.shape[-1]
    assert depth == 1 and d == 1024 and seq % RET_CHUNK == 0
    assert w_in.shape[-1] == 10 * 1024 and w_gate_up.shape[-1] == 2 * d
    rows = batch * seq

    tm_in = _largest_tile(seq, 1024)
    tt_ret = _largest_tile(seq, 512)
    tt_tail = _largest_tile(seq, 256)
    td = _largest_tile(rows, 512)
    tc = _largest_tile(rows, 512)
    tm_e = 512

    x2d = x.reshape(rows, d)
    w_in_bf = w_in[0].astype(BF16)
    row1 = lambda v: v.reshape(1, -1)

    cos, sin = _rope_tables(N_META + jnp.arange(seq))
    zz = _inproj(x2d, row1(norm_mix_w[0]), w_in_bf, cos, sin, tm_in)
    pad = RET_CHUNK - N_META
    meta_chunk = jnp.concatenate([jnp.zeros((pad, d), x.dtype), meta_tokens.astype(x.dtype)], axis=0)
    cos_m, sin_m = _rope_tables(jnp.maximum(jnp.arange(RET_CHUNK) - pad, 0))
    zm = _inproj(meta_chunk, row1(norm_mix_w[0]), w_in_bf, cos_m, sin_m, RET_CHUNK)

    yg = _retention(zz, zm, _retention_tables(), row1(ret_gn_w[0]), batch, seq, tt_ret)

    tail_params = dict(
        dw_w=conv_dw_w[0], dw_b=row1(conv_dw_b[0]), ln_w=row1(conv_ln_w[0]), ln_b=row1(conv_ln_b[0]),
        w_conv_out=w_conv_out[0].astype(BF16), b_conv_out=row1(b_conv_out[0]),
        w_ret_out=w_ret_out[0].astype(BF16), w_mix_out=w_mix_out[0].astype(BF16),
        norm_ffn_w=row1(norm_ffn_w[0]), w_router_t=w_router[0].T, b_router=b_router[0].reshape(-1, 1))
    h2, xn, topi, gates, rank, counts = _tail(x2d, zz, yg, zm, tail_params, batch, seq, tt_tail, n_experts)

    counts = counts[:, 0].astype(I32)
    pcounts = (counts + tm_e - 1) // tm_e * tm_e
    pends = jnp.cumsum(pcounts).astype(I32)
    pstarts = pends - pcounts
    dest = pstarts[topi[:TOP_K]] + rank[:TOP_K]
    n_tiles = rows * TOP_K // tm_e + n_experts
    tile_e = jnp.minimum(jnp.searchsorted(pends, jnp.arange(n_tiles, dtype=I32) * tm_e, side="right"),
                         n_experts - 1).astype(I32)
    n_used = (pends[-1:] // tm_e).astype(I32)

    xs = _dispatch(pends, pcounts, _tile_major(dest, td), xn, n_tiles * tm_e, td, tm_e, n_experts)
    ys = _experts(tile_e, n_used, xs, w_gate_up[0].astype(BF16), b_gate_up[0][:, None, :],
                  w_down[0].astype(BF16), b_down[0][:, None, :], tm_e)
    out = _combine(_tile_major(dest, tc), ys, h2, gates, row1(norm_final_w), tc)
    return out.reshape(batch, seq, d)
```

```python
import functools

import jax
import jax.numpy as jnp
import numpy as np
from jax import lax
from jax.experimental import pallas as pl
from jax.experimental.pallas import tpu as pltpu

N_META = 16
CONV_WIDTH = 31
RET_HEADS = 4
RET_DK = 256
RET_DV = 512
RET_CHUNK = 128
ROPE_BASE = 10000.0
TOP_K = 4
SWIGLU_LIMIT = 7.0
SWIGLU_ALPHA = 1.702
EPS = 1e-5

LANES = 128
SUBLANES = 8
CONV_HALO = 32
VMEM_LIMIT = 56 * 1024 * 1024

F32 = jnp.float32
BF16 = jnp.bfloat16
I32 = jnp.int32

WB_CONV_A, WB_CONV_B, WB_Q, WB_V, WB_G, WB_GATE = 0, 1, 2, 4, 6, 8


def _params(n_grid_axes):
    return pltpu.CompilerParams(dimension_semantics=("arbitrary",) * n_grid_axes,
                                vmem_limit_bytes=VMEM_LIMIT)


def _inproj_glu_body(x_ref, nw_ref, wa_ref, wb_ref, u_ref, hn_ref):
    x = x_ref[...]
    ms = jnp.mean(x * x, axis=-1, keepdims=True)
    hn = (x * lax.rsqrt(ms + EPS) * nw_ref[...]).astype(BF16)
    hn_ref[...] = hn
    a = jnp.dot(hn, wa_ref[...], preferred_element_type=F32)
    b = jnp.dot(hn, wb_ref[...], preferred_element_type=F32)
    u_ref[...] = (a * jax.nn.sigmoid(b)).astype(BF16)


def _inproj_glu(x2d, norm_w, w_in_bf, tm):
    rows, d = x2d.shape
    return pl.pallas_call(
        _inproj_glu_body,
        out_shape=(jax.ShapeDtypeStruct((rows, 1024), BF16), jax.ShapeDtypeStruct((rows, d), BF16)),
        grid=(rows // tm,),
        in_specs=[
            pl.BlockSpec((tm, d), lambda i: (i, 0)),
            pl.BlockSpec((1, d), lambda i: (0, 0)),
            pl.BlockSpec((d, 1024), lambda i: (0, WB_CONV_A)),
            pl.BlockSpec((d, 1024), lambda i: (0, WB_CONV_B)),
        ],
        out_specs=(pl.BlockSpec((tm, 1024), lambda i: (i, 0)), pl.BlockSpec((tm, d), lambda i: (i, 0))),
        compiler_params=_params(1),
        name="inproj_glu",
    )(x2d, norm_w, w_in_bf, w_in_bf)


def _inproj_rotary_body(hn_ref, w_ref, cos_ref, sin_ref, o_ref):
    acc = jnp.dot(hn_ref[...], w_ref[...], preferred_element_type=F32)
    scale = jnp.where(pl.program_id(1) == 0, RET_DK ** -0.5, 1.0).astype(F32)
    cos = cos_ref[...] * scale
    sin = sin_ref[...] * scale
    half = RET_DK // 2
    for h in range(RET_HEADS):
        x1 = acc[:, h * RET_DK:h * RET_DK + half]
        x2 = acc[:, h * RET_DK + half:(h + 1) * RET_DK]
        o_ref[:, h * RET_DK:h * RET_DK + half] = (x1 * cos - x2 * sin).astype(BF16)
        o_ref[:, h * RET_DK + half:(h + 1) * RET_DK] = (x2 * cos + x1 * sin).astype(BF16)


def _inproj_rotary(hn, w_in_bf, cos, sin, tm):
    rows, d = hn.shape
    period_tiles = cos.shape[0] // tm
    return pl.pallas_call(
        _inproj_rotary_body,
        out_shape=jax.ShapeDtypeStruct((rows, 2048), BF16),
        grid=(rows // tm, 2),
        in_specs=[
            pl.BlockSpec((tm, d), lambda i, j: (i, 0)),
            pl.BlockSpec((d, 1024), lambda i, j: (0, WB_Q + j)),
            pl.BlockSpec((tm, LANES), lambda i, j: (i % period_tiles, 0)),
            pl.BlockSpec((tm, LANES), lambda i, j: (i % period_tiles, 0)),
        ],
        out_specs=pl.BlockSpec((tm, 1024), lambda i, j: (i, j)),
        compiler_params=_params(2),
        name="inproj_rotary",
    )(hn, w_in_bf, cos, sin)


def _inproj_elementwise_body(hn_ref, w_ref, o_ref, *, fn):
    o_ref[...] = fn(jnp.dot(hn_ref[...], w_ref[...], preferred_element_type=F32)).astype(BF16)


def _silu(x):
    return x * jax.nn.sigmoid(x)


def _inproj_elementwise(hn, w_in_bf, first_block, fn, tm, name):
    rows, d = hn.shape
    return pl.pallas_call(
        functools.partial(_inproj_elementwise_body, fn=fn),
        out_shape=jax.ShapeDtypeStruct((rows, 2048), BF16),
        grid=(rows // tm, 2),
        in_specs=[
            pl.BlockSpec((tm, d), lambda i, j: (i, 0)),
            pl.BlockSpec((d, 1024), lambda i, j: (0, first_block + j)),
        ],
        out_specs=pl.BlockSpec((tm, 1024), lambda i, j: (i, j)),
        compiler_params=_params(2),
        name=name,
    )(hn, w_in_bf)


def _dot_t0(a, b):
    return lax.dot_general(a, b, (((0,), (0,)), ((), ())), preferred_element_type=F32)


def _dot_t1(a, b):
    return lax.dot_general(a, b, (((1,), (1,)), ((), ())), preferred_element_type=F32)


def _retention_body(cd_ref, q_ref, k_ref, v_ref, sg_ref, km_ref, vm_ref, dmat_ref, zeta_ref, xi_ref, gnw_ref,
                    o_ref, state_ref, *, n_chunks):
    t = pl.program_id(1)
    C, DK, DV = RET_CHUNK, RET_DK, RET_DV

    def state_update(prev, h, k_h, v_h):
        kz = (k_h.astype(F32) * zeta_ref[h]).astype(BF16)
        return prev * cd_ref[h] + _dot_t0(kz, v_h)

    @pl.when(t == 0)
    def _seed_from_meta_chunk():
        for h in range(RET_HEADS):
            state_ref[h] = state_update(jnp.zeros((DK, DV), F32), h,
                                        km_ref[:, h * DK:(h + 1) * DK], vm_ref[:, h * DV:(h + 1) * DV])

    def chunk(c, carry):
        rows = pl.ds(pl.multiple_of(c * C, C), C)
        for h in range(RET_HEADS):
            q_h = q_ref[rows, h * DK:(h + 1) * DK]
            k_h = k_ref[rows, h * DK:(h + 1) * DK]
            v_h = v_ref[rows, h * DV:(h + 1) * DV]
            scores = _dot_t1(q_h, k_h) * dmat_ref[h]
            inner = jnp.dot(scores.astype(BF16), v_h, preferred_element_type=F32)
            state = state_ref[h]
            cross = jnp.dot(q_h, state.astype(BF16), preferred_element_type=F32) * xi_ref[h]
            state_ref[h] = state_update(state, h, k_h, v_h)
            y = inner + cross
            mu = jnp.mean(y, axis=-1, keepdims=True)
            yc = y - mu
            yn = yc * lax.rsqrt(jnp.mean(yc * yc, axis=-1, keepdims=True) + EPS)
            gated = yn * gnw_ref[:, h * DV:(h + 1) * DV] * sg_ref[rows, h * DV:(h + 1) * DV].astype(F32)
            o_ref[rows, h * DV:(h + 1) * DV] = gated.astype(BF16)
        return carry

    lax.fori_loop(0, n_chunks, chunk, 0)


def _retention(qk, v, sg, qk_meta, v_meta, tables, gn_w, batch, seq, tt):
    chunk_decay, dmat, zeta, xi = tables
    tiles = seq // tt
    row = lambda b, t: b * tiles + t
    hv = RET_HEADS * RET_DV
    full = lambda shape: pl.BlockSpec(shape, lambda b, t: (0,) * len(shape))
    return pl.pallas_call(
        functools.partial(_retention_body, n_chunks=tt // RET_CHUNK),
        out_shape=jax.ShapeDtypeStruct((batch * seq, hv), BF16),
        grid=(batch, tiles),
        in_specs=[
            pl.BlockSpec(memory_space=pltpu.SMEM),
            pl.BlockSpec((tt, 1024), lambda b, t: (row(b, t), 0)),
            pl.BlockSpec((tt, 1024), lambda b, t: (row(b, t), 1)),
            pl.BlockSpec((tt, hv), lambda b, t: (row(b, t), 0)),
            pl.BlockSpec((tt, hv), lambda b, t: (row(b, t), 0)),
            pl.BlockSpec((RET_CHUNK, 1024), lambda b, t: (0, 1)),
            full((RET_CHUNK, hv)),
            full((RET_HEADS, RET_CHUNK, RET_CHUNK)),
            full((RET_HEADS, RET_CHUNK, 1)),
            full((RET_HEADS, RET_CHUNK, 1)),
            full((1, hv)),
        ],
        out_specs=pl.BlockSpec((tt, hv), lambda b, t: (row(b, t), 0)),
        scratch_shapes=[pltpu.VMEM((RET_HEADS, RET_DK, RET_DV), F32)],
        compiler_params=_params(2),
        name="retention",
    )(chunk_decay, qk, qk, v, sg, qk_meta, v_meta, dmat, zeta, xi, gn_w)


def _tail_body(x_ref, u_ref, gates_ref, yg_ref, um_ref, dww_ref, dwb_ref, lnw_ref, lnb_ref,
               wco_ref, bco_ref, wro_ref, wmo_ref, nfw_ref, wrt_ref, br_ref, tri_ref,
               h2_ref, xn_ref, topi_ref, gate_ref, rank_ref, cnt_ref,
               uext_ref, phase_ref, conv_ref, count_ref, *, tt, n_experts):
    b = pl.program_id(0)
    t = pl.program_id(1)
    C = RET_CHUNK
    d = x_ref.shape[1]

    @pl.when((b == 0) & (t == 0))
    def _zero_counts():
        count_ref[...] = jnp.zeros_like(count_ref)

    @pl.when(t == 0)
    def _seed_conv_history():
        uext_ref[0:CONV_HALO, :] = um_ref[C - CONV_HALO:C, :].astype(F32)

    uext_ref[CONV_HALO:CONV_HALO + tt, :] = u_ref[...].astype(F32)

    first = CONV_HALO - (CONV_WIDTH - 1)
    for s in range(d // LANES):
        cols = slice(s * LANES, (s + 1) * LANES)
        for phase in range(1, SUBLANES):
            phase_ref[phase - 1] = uext_ref[phase:phase + phase_ref.shape[1], cols]
        for c in range(tt // C):
            acc = jnp.zeros((C, LANES), F32)
            for k in range(CONV_WIDTH):
                a, phase = divmod(first + k, SUBLANES)
                r0 = c * C + a * SUBLANES
                window = uext_ref[r0:r0 + C, cols] if phase == 0 else phase_ref[phase - 1, r0:r0 + C, :]
                acc = acc + dww_ref[k:k + 1, cols] * window
            conv_ref[c * C:(c + 1) * C, cols] = acc + dwb_ref[:, cols]
    uext_ref[0:CONV_HALO, :] = uext_ref[tt:tt + CONV_HALO, :]

    cv = conv_ref[...]
    mu = jnp.mean(cv, axis=-1, keepdims=True)
    cc = cv - mu
    ln = cc * lax.rsqrt(jnp.mean(cc * cc, axis=-1, keepdims=True) + EPS) * lnw_ref[...] + lnb_ref[...]
    act = (ln * jax.nn.sigmoid(ln)).astype(BF16)
    y_conv = jnp.dot(act, wco_ref[...], preferred_element_type=F32) + bco_ref[...]
    y_ret = jnp.dot(yg_ref[...], wro_ref[...], preferred_element_type=F32)
    merged = (gates_ref[:, :d].astype(F32) * y_conv + gates_ref[:, d:].astype(F32) * y_ret).astype(BF16)
    h2 = x_ref[...] + jnp.dot(merged, wmo_ref[...], preferred_element_type=F32)
    h2_ref[...] = h2

    xn = h2 * lax.rsqrt(jnp.mean(h2 * h2, axis=-1, keepdims=True) + EPS) * nfw_ref[...]
    xn_ref[...] = xn

    logits = lax.dot_general(wrt_ref[...], xn, (((1,), (1,)), ((), ())),
                             precision=lax.Precision.HIGHEST, preferred_element_type=F32) + br_ref[...]
    eidx = lax.broadcasted_iota(I32, (n_experts, tt), 0)
    selected = jnp.zeros((n_experts, tt), F32)
    vals, idxs = [], []
    for _ in range(TOP_K):
        m = jnp.max(logits, axis=0, keepdims=True)
        idx = jnp.min(jnp.where(logits == m, eidx, n_experts), axis=0, keepdims=True)
        hit = eidx == idx
        vals.append(m)
        idxs.append(idx)
        selected = jnp.where(hit, 1.0, selected)
        logits = jnp.where(hit, -jnp.inf, logits)
    exps = [jnp.exp(v - vals[0]) for v in vals]
    denom = exps[0] + exps[1] + exps[2] + exps[3]

    before = jnp.dot(selected.astype(BF16), tri_ref[...], preferred_element_type=F32) + count_ref[:, 0:1]
    ranks = [jnp.sum(jnp.where(eidx == idx, before, 0.0), axis=0, keepdims=True).astype(I32) for idx in idxs]
    count_ref[...] = count_ref[...] + jnp.sum(selected, axis=1, keepdims=True)

    zi = jnp.zeros((8 - TOP_K, tt), I32)
    topi_ref[...] = jnp.concatenate(idxs + [zi], axis=0)
    rank_ref[...] = jnp.concatenate(ranks + [zi], axis=0)
    gate_ref[...] = jnp.concatenate([e / denom for e in exps] + [zi.astype(F32)], axis=0)
    cnt_ref[...] = count_ref[...]


def _tail(x2d, u, gates, yg, u_meta, p, batch, seq, tt, n_experts):
    d = x2d.shape[1]
    tiles = seq // tt
    rows = batch * seq
    row = lambda b, t: b * tiles + t
    full = lambda shape: pl.BlockSpec(shape, lambda b, t: (0,) * len(shape), pipeline_mode=pl.Buffered(1))
    tri = (jnp.arange(tt)[:, None] < jnp.arange(tt)[None, :]).astype(BF16)
    lane_rows = pl.BlockSpec((8, tt), lambda b, t: (0, row(b, t)))
    return pl.pallas_call(
        functools.partial(_tail_body, tt=tt, n_experts=n_experts),
        out_shape=(
            jax.ShapeDtypeStruct((rows, d), F32),
            jax.ShapeDtypeStruct((rows, d), F32),
            jax.ShapeDtypeStruct((8, rows), I32),
            jax.ShapeDtypeStruct((8, rows), F32),
            jax.ShapeDtypeStruct((8, rows), I32),
            jax.ShapeDtypeStruct((n_experts, LANES), F32),
        ),
        grid=(batch, tiles),
        in_specs=[
            pl.BlockSpec((tt, d), lambda b, t: (row(b, t), 0)),
            pl.BlockSpec((tt, 1024), lambda b, t: (row(b, t), 0)),
            pl.BlockSpec((tt, 2048), lambda b, t: (row(b, t), 0)),
            pl.BlockSpec((tt, yg.shape[1]), lambda b, t: (row(b, t), 0)),
            full((RET_CHUNK, 1024)),
            full((CONV_WIDTH, d)), full((1, d)), full((1, d)), full((1, d)),
            full((d, d)), full((1, d)),
            full((yg.shape[1], d)),
            full((d, d)),
            full((1, d)),
            full((n_experts, d)), full((n_experts, 1)),
            full((tt, tt)),
        ],
        out_specs=(
            pl.BlockSpec((tt, d), lambda b, t: (row(b, t), 0)),
            pl.BlockSpec((tt, d), lambda b, t: (row(b, t), 0)),
            lane_rows, lane_rows, lane_rows,
            pl.BlockSpec((n_experts, LANES), lambda b, t: (0, 0)),
        ),
        scratch_shapes=[
            pltpu.VMEM((CONV_HALO + tt, d), F32),
            pltpu.VMEM((SUBLANES - 1, CONV_HALO + tt - SUBLANES, LANES), F32),
            pltpu.VMEM((tt, d), F32),
            pltpu.VMEM((n_experts, LANES), F32),
        ],
        compiler_params=_params(2),
        name="mixer_tail",
    )(x2d, u, gates, yg, u_meta, p["dw_w"], p["dw_b"], p["ln_w"], p["ln_b"], p["w_conv_out"], p["b_conv_out"],
      p["w_ret_out"], p["w_mix_out"], p["norm_ffn_w"], p["w_router_t"], p["b_router"], tri)


def _dispatch_body(pend_ref, pcnt_ref, dest_ref, xn_ref, xs_hbm, zero_ref, sem, zsem, *, td, tm, n_experts):
    def zero_copy(e):
        start = pl.multiple_of(pend_ref[e] - tm, tm)
        return pltpu.make_async_copy(zero_ref, xs_hbm.at[pl.ds(start, tm)], zsem)

    @pl.when(pl.program_id(0) == 0)
    def _zero_last_tile_of_every_expert():
        zero_ref[...] = jnp.zeros_like(zero_ref)
        for e in range(n_experts):
            pl.when(pcnt_ref[e] > 0)(lambda e=e: zero_copy(e).start())
        for e in range(n_experts):
            pl.when(pcnt_ref[e] > 0)(lambda e=e: zero_copy(e).wait())

    def row_copy(r, dst_row):
        return pltpu.make_async_copy(xn_ref.at[pl.ds(r, 1)], xs_hbm.at[pl.ds(dst_row, 1)], sem)

    def issue(r, carry):
        for k in range(TOP_K):
            row_copy(r, dest_ref[k * td + r]).start()
        return carry

    lax.fori_loop(0, td, issue, 0, unroll=8)

    def drain(r, carry):
        for k in range(TOP_K):
            row_copy(0, 0).wait()
        return carry

    lax.fori_loop(0, td, drain, 0, unroll=8)


def _dispatch(pends, pcounts, dest_tiles, xn, n_rows_out, td, tm, n_experts):
    rows, w = xn.shape
    return pl.pallas_call(
        functools.partial(_dispatch_body, td=td, tm=tm, n_experts=n_experts),
        out_shape=jax.ShapeDtypeStruct((n_rows_out, w), F32),
        grid_spec=pltpu.PrefetchScalarGridSpec(
            num_scalar_prefetch=2,
            grid=(rows // td,),
            in_specs=[
                pl.BlockSpec((TOP_K * td,), lambda i, *_: (i,), memory_space=pltpu.SMEM),
                pl.BlockSpec((td, w), lambda i, *_: (i, 0)),
            ],
            out_specs=pl.BlockSpec(memory_space=pl.ANY),
            scratch_shapes=[pltpu.VMEM((tm, w), F32), pltpu.SemaphoreType.DMA, pltpu.SemaphoreType.DMA],
        ),
        compiler_params=_params(1),
        name="moe_dispatch",
    )(pends, pcounts, dest_tiles, xn)


def _experts_body(tile_e_ref, n_used_ref, xs_ref, wgu_ref, bgu_ref, wd_ref, bd_ref, ys_ref):
    @pl.when(pl.program_id(0) < n_used_ref[0])
    def _():
        x = xs_ref[...].astype(BF16)
        hgu = jnp.dot(x, wgu_ref[0], preferred_element_type=F32) + bgu_ref[0]
        de = hgu.shape[1] // 2
        h_gate = jnp.minimum(hgu[:, :de], SWIGLU_LIMIT)
        h_up = jnp.clip(hgu[:, de:], -SWIGLU_LIMIT, SWIGLU_LIMIT)
        act = h_gate * jax.nn.sigmoid(SWIGLU_ALPHA * h_gate) * (h_up + 1.0)
        y = jnp.dot(act.astype(BF16), wd_ref[0], preferred_element_type=F32) + bd_ref[0]
        ys_ref[...] = y


def _experts(tile_e, n_used, xs, wgu, bgu, wd, bd, tm):
    n_rows, w = xs.shape
    e, d, de2 = wgu.shape
    tile = lambda p, te, nu: jnp.minimum(p, nu[0] - 1)
    expert = lambda p, te, nu: te[tile(p, te, nu)]
    return pl.pallas_call(
        _experts_body,
        out_shape=jax.ShapeDtypeStruct((n_rows, w), F32),
        grid_spec=pltpu.PrefetchScalarGridSpec(
            num_scalar_prefetch=2,
            grid=(n_rows // tm,),
            in_specs=[
                pl.BlockSpec((tm, w), lambda p, te, nu: (tile(p, te, nu), 0)),
                pl.BlockSpec((1, d, de2), lambda p, te, nu: (expert(p, te, nu), 0, 0)),
                pl.BlockSpec((1, 1, de2), lambda p, te, nu: (expert(p, te, nu), 0, 0)),
                pl.BlockSpec((1, de2 // 2, d), lambda p, te, nu: (expert(p, te, nu), 0, 0)),
                pl.BlockSpec((1, 1, d), lambda p, te, nu: (expert(p, te, nu), 0, 0)),
            ],
            out_specs=pl.BlockSpec((tm, w), lambda p, te, nu: (tile(p, te, nu), 0)),
        ),
        compiler_params=_params(1),
        name="moe_experts",
    )(tile_e, n_used, xs, wgu, bgu, wd, bd)


def _combine_body(dest_ref, ys_hbm, h2_ref, gate_ref, nw_ref, o_ref, buf_ref, sem, *, tc):
    def row_copy(src_row, k, r):
        return pltpu.make_async_copy(ys_hbm.at[pl.ds(src_row, 1)], buf_ref.at[k, pl.ds(r, 1)], sem)

    def issue(r, carry):
        for k in range(TOP_K):
            row_copy(dest_ref[k * tc + r], k, r).start()
        return carry

    lax.fori_loop(0, tc, issue, 0, unroll=8)

    g = gate_ref[...]
    g_cols = jnp.concatenate([g, jnp.zeros((LANES - g.shape[0], tc), F32)], axis=0).T

    def drain(r, carry):
        for k in range(TOP_K):
            row_copy(0, k, 0).wait()
        return carry

    lax.fori_loop(0, tc, drain, 0, unroll=8)

    h = h2_ref[...]
    for k in range(TOP_K):
        h = h + g_cols[:, k:k + 1] * buf_ref[k]
    o_ref[...] = h * lax.rsqrt(jnp.mean(h * h, axis=-1, keepdims=True) + EPS) * nw_ref[...]


def _combine(dest_tiles, ys, h2, gates, norm_w, tc):
    rows, d = h2.shape
    return pl.pallas_call(
        functools.partial(_combine_body, tc=tc),
        out_shape=jax.ShapeDtypeStruct((rows, d), F32),
        grid=(rows // tc,),
        in_specs=[
            pl.BlockSpec((TOP_K * tc,), lambda i: (i,), memory_space=pltpu.SMEM),
            pl.BlockSpec(memory_space=pl.ANY),
            pl.BlockSpec((tc, d), lambda i: (i, 0)),
            pl.BlockSpec((8, tc), lambda i: (0, i)),
            pl.BlockSpec((1, d), lambda i: (0, 0)),
        ],
        out_specs=pl.BlockSpec((tc, d), lambda i: (i, 0)),
        scratch_shapes=[pltpu.VMEM((TOP_K, tc, d), F32), pltpu.SemaphoreType.DMA],
        compiler_params=_params(1),
        name="moe_combine",
    )(dest_tiles, ys, h2, gates, norm_w)


def _rope_tables(positions):
    inv_freq = ROPE_BASE ** (-np.arange(0, RET_DK, 2, dtype=np.float64) / RET_DK)
    ang = np.asarray(positions, np.float64)[:, None] * inv_freq[None, :]
    return jnp.asarray(np.cos(ang), F32), jnp.asarray(np.sin(ang), F32)


def _retention_tables():
    c = RET_CHUNK
    log_gamma = jnp.log1p(-jnp.exp2(-5.0 - jnp.arange(RET_HEADS, dtype=F32)))
    idx = jnp.arange(c, dtype=F32)
    diff = idx[:, None] - idx[None, :]
    dmat = jnp.where(diff[None] >= 0, jnp.exp(log_gamma[:, None, None] * jnp.maximum(diff, 0.0)[None]), 0.0)
    zeta = jnp.exp(log_gamma[:, None] * (c - 1 - idx)[None, :])[:, :, None]
    xi = jnp.exp(log_gamma[:, None] * (idx + 1.0)[None, :])[:, :, None]
    chunk_decay = jnp.exp(log_gamma * c)
    return chunk_decay, dmat, zeta, xi


def _largest_tile(n, cap):
    t = cap
    while n % t:
        t //= 2
    return t


def _tile_major(a, tile):
    k, t = a.shape
    return a.reshape(k, t // tile, tile).transpose(1, 0, 2).reshape(-1)


def kernel(x, meta_tokens, norm_mix_w, w_in, conv_dw_w, conv_dw_b, conv_ln_w, conv_ln_b, w_conv_out,
           b_conv_out, ret_gn_w, w_ret_out, w_mix_out, norm_ffn_w, w_router, b_router, w_gate_up,
           b_gate_up, w_down, b_down, norm_final_w):
    batch, seq, d = x.shape
    depth = w_in.shape[0]
    n_experts = w_router.shape[-1]
    assert depth == 1 and d == 1024 and seq % RET_CHUNK == 0
    assert w_in.shape[-1] == 10 * 1024 and w_gate_up.shape[-1] == 2 * d
    rows = batch * seq

    tm_in = _largest_tile(seq, 1024)
    tt_ret = _largest_tile(seq, 512)
    tt_tail = _largest_tile(seq, 512)
    td = _largest_tile(rows, 512)
    tc = _largest_tile(rows, 512)
    tm_e = 512

    x2d = x.reshape(rows, d)
    w_in_bf = w_in[0].astype(BF16)
    row1 = lambda v: v.reshape(1, -1)
    norm_w = row1(norm_mix_w[0])

    cos, sin = _rope_tables(N_META + np.arange(seq))
    u, hn = _inproj_glu(x2d, norm_w, w_in_bf, tm_in)
    qk = _inproj_rotary(hn, w_in_bf, cos, sin, tm_in)
    v = _inproj_elementwise(hn, w_in_bf, WB_V, lambda z: z, tm_in, "inproj_value")
    sg = _inproj_elementwise(hn, w_in_bf, WB_G, _silu, tm_in, "inproj_silu")
    gates = _inproj_elementwise(hn, w_in_bf, WB_GATE, jax.nn.sigmoid, tm_in, "inproj_gate")

    pad = RET_CHUNK - N_META
    meta_chunk = jnp.concatenate([jnp.zeros((pad, d), x.dtype), meta_tokens.astype(x.dtype)], axis=0)
    cos_m, sin_m = _rope_tables(np.maximum(np.arange(RET_CHUNK) - pad, 0))
    u_meta, hn_meta = _inproj_glu(meta_chunk, norm_w, w_in_bf, RET_CHUNK)
    qk_meta = _inproj_rotary(hn_meta, w_in_bf, cos_m, sin_m, RET_CHUNK)
    v_meta = _inproj_elementwise(hn_meta, w_in_bf, WB_V, lambda z: z, RET_CHUNK, "inproj_value_meta")

    yg = _retention(qk, v, sg, qk_meta, v_meta, _retention_tables(), row1(ret_gn_w[0]), batch, seq, tt_ret)

    tail_params = dict(
        dw_w=conv_dw_w[0], dw_b=row1(conv_dw_b[0]), ln_w=row1(conv_ln_w[0]), ln_b=row1(conv_ln_b[0]),
        w_conv_out=w_conv_out[0].astype(BF16), b_conv_out=row1(b_conv_out[0]),
        w_ret_out=w_ret_out[0].astype(BF16), w_mix_out=w_mix_out[0].astype(BF16),
        norm_ffn_w=row1(norm_ffn_w[0]), w_router_t=w_router[0].T, b_router=b_router[0].reshape(-1, 1))
    h2, xn, topi, gate_w, rank, counts = _tail(x2d, u, gates, yg, u_meta, tail_params, batch, seq, tt_tail,
                                               n_experts)

    counts = counts[:, 0].astype(I32)
    pcounts = (counts + tm_e - 1) // tm_e * tm_e
    pends = jnp.cumsum(pcounts).astype(I32)
    pstarts = pends - pcounts
    experts = jnp.arange(n_experts, dtype=I32)
    start_of = jnp.sum(jnp.where(topi[:TOP_K, :, None] == experts, pstarts, 0), axis=-1)
    dest = start_of + rank[:TOP_K]
    n_tiles = rows * TOP_K // tm_e + n_experts
    tile_starts = jnp.arange(n_tiles, dtype=I32) * tm_e
    tile_e = jnp.minimum(jnp.sum(pends[None, :] <= tile_starts[:, None], axis=-1), n_experts - 1).astype(I32)
    n_used = (pends[-1:] // tm_e).astype(I32)

    xs = _dispatch(pends, pcounts, _tile_major(dest, td), xn, n_tiles * tm_e, td, tm_e, n_experts)
    ys = _experts(tile_e, n_used, xs, w_gate_up[0].astype(BF16), b_gate_up[0][:, None, :],
                  w_down[0].astype(BF16), b_down[0][:, None, :], tm_e)
    out = _combine(_tile_major(dest, tc), ys, h2, gate_w, row1(norm_final_w), tc)
    return out.reshape(batch, seq, d)
```

```python
import functools

import jax
import jax.numpy as jnp
import numpy as np
from jax import lax
from jax.experimental import pallas as pl
from jax.experimental.pallas import tpu as pltpu

N_META = 16
CONV_WIDTH = 31
RET_HEADS = 4
RET_DK = 256
RET_DV = 512
RET_CHUNK = 128
ROPE_BASE = 10000.0
TOP_K = 4
SWIGLU_LIMIT = 7.0
SWIGLU_ALPHA = 1.702
EPS = 1e-5

LANES = 128
SUBLANES = 8
CONV_HALO = 32
DMA_PRIORITIES = 2
VMEM_LIMIT = 56 * 1024 * 1024

F32 = jnp.float32
BF16 = jnp.bfloat16
I32 = jnp.int32

WB_CONV_A, WB_CONV_B, WB_Q, WB_V, WB_G, WB_GATE = 0, 1, 2, 4, 6, 8


def _params(n_grid_axes):
    return pltpu.CompilerParams(dimension_semantics=("arbitrary",) * n_grid_axes,
                                vmem_limit_bytes=VMEM_LIMIT)


def _inproj_glu_body(x_ref, nw_ref, wa_ref, wb_ref, u_ref, hn_ref):
    x = x_ref[...]
    ms = jnp.mean(x * x, axis=-1, keepdims=True)
    hn = (x * lax.rsqrt(ms + EPS) * nw_ref[...]).astype(BF16)
    hn_ref[...] = hn
    a = jnp.dot(hn, wa_ref[...], preferred_element_type=F32)
    b = jnp.dot(hn, wb_ref[...], preferred_element_type=F32)
    u_ref[...] = (a * jax.nn.sigmoid(b)).astype(BF16)


def _inproj_glu(x2d, norm_w, w_in_bf, tm):
    rows, d = x2d.shape
    return pl.pallas_call(
        _inproj_glu_body,
        out_shape=(jax.ShapeDtypeStruct((rows, 1024), BF16), jax.ShapeDtypeStruct((rows, d), BF16)),
        grid=(rows // tm,),
        in_specs=[
            pl.BlockSpec((tm, d), lambda i: (i, 0)),
            pl.BlockSpec((1, d), lambda i: (0, 0)),
            pl.BlockSpec((d, 1024), lambda i: (0, WB_CONV_A)),
            pl.BlockSpec((d, 1024), lambda i: (0, WB_CONV_B)),
        ],
        out_specs=(pl.BlockSpec((tm, 1024), lambda i: (i, 0)), pl.BlockSpec((tm, d), lambda i: (i, 0))),
        compiler_params=_params(1),
        name="inproj_glu",
    )(x2d, norm_w, w_in_bf, w_in_bf)


def _inproj_rotary_body(hn_ref, w_ref, cos_ref, sin_ref, o_ref):
    acc = jnp.dot(hn_ref[...], w_ref[...], preferred_element_type=F32)
    scale = jnp.where(pl.program_id(1) == 0, RET_DK ** -0.5, 1.0).astype(F32)
    cos = cos_ref[...] * scale
    sin = sin_ref[...] * scale
    half = RET_DK // 2
    for h in range(RET_HEADS):
        x1 = acc[:, h * RET_DK:h * RET_DK + half]
        x2 = acc[:, h * RET_DK + half:(h + 1) * RET_DK]
        o_ref[:, h * RET_DK:h * RET_DK + half] = (x1 * cos - x2 * sin).astype(BF16)
        o_ref[:, h * RET_DK + half:(h + 1) * RET_DK] = (x2 * cos + x1 * sin).astype(BF16)


def _inproj_rotary(hn, w_in_bf, cos, sin, tm):
    rows, d = hn.shape
    period_tiles = cos.shape[0] // tm
    return pl.pallas_call(
        _inproj_rotary_body,
        out_shape=jax.ShapeDtypeStruct((rows, 2048), BF16),
        grid=(rows // tm, 2),
        in_specs=[
            pl.BlockSpec((tm, d), lambda i, j: (i, 0)),
            pl.BlockSpec((d, 1024), lambda i, j: (0, WB_Q + j)),
            pl.BlockSpec((tm, LANES), lambda i, j: (i % period_tiles, 0)),
            pl.BlockSpec((tm, LANES), lambda i, j: (i % period_tiles, 0)),
        ],
        out_specs=pl.BlockSpec((tm, 1024), lambda i, j: (i, j)),
        compiler_params=_params(2),
        name="inproj_rotary",
    )(hn, w_in_bf, cos, sin)


def _inproj_elementwise_body(hn_ref, w_ref, o_ref, *, fn):
    o_ref[...] = fn(jnp.dot(hn_ref[...], w_ref[...], preferred_element_type=F32)).astype(BF16)


def _silu(x):
    return x * jax.nn.sigmoid(x)


def _inproj_elementwise(hn, w_in_bf, first_block, fn, tm, name):
    rows, d = hn.shape
    return pl.pallas_call(
        functools.partial(_inproj_elementwise_body, fn=fn),
        out_shape=jax.ShapeDtypeStruct((rows, 2048), BF16),
        grid=(rows // tm, 2),
        in_specs=[
            pl.BlockSpec((tm, d), lambda i, j: (i, 0)),
            pl.BlockSpec((d, 1024), lambda i, j: (0, first_block + j)),
        ],
        out_specs=pl.BlockSpec((tm, 1024), lambda i, j: (i, j)),
        compiler_params=_params(2),
        name=name,
    )(hn, w_in_bf)


def _dot_t0(a, b):
    return lax.dot_general(a, b, (((0,), (0,)), ((), ())), preferred_element_type=F32)


def _dot_t1(a, b):
    return lax.dot_general(a, b, (((1,), (1,)), ((), ())), preferred_element_type=F32)


def _retention_body(cd_ref, q_ref, k_ref, v_ref, sg_ref, km_ref, vm_ref, dmat_ref, zeta_ref, xi_ref, gnw_ref,
                    o_ref, state_ref, *, n_chunks):
    t = pl.program_id(1)
    C, DK, DV = RET_CHUNK, RET_DK, RET_DV

    def state_update(prev, h, k_h, v_h):
        kz = (k_h.astype(F32) * zeta_ref[h]).astype(BF16)
        return prev * cd_ref[h] + _dot_t0(kz, v_h)

    @pl.when(t == 0)
    def _seed_from_meta_chunk():
        for h in range(RET_HEADS):
            state_ref[h] = state_update(jnp.zeros((DK, DV), F32), h,
                                        km_ref[:, h * DK:(h + 1) * DK], vm_ref[:, h * DV:(h + 1) * DV])

    def chunk(c, carry):
        rows = pl.ds(pl.multiple_of(c * C, C), C)
        for h in range(RET_HEADS):
            q_h = q_ref[rows, h * DK:(h + 1) * DK]
            k_h = k_ref[rows, h * DK:(h + 1) * DK]
            v_h = v_ref[rows, h * DV:(h + 1) * DV]
            scores = _dot_t1(q_h, k_h) * dmat_ref[h]
            inner = jnp.dot(scores.astype(BF16), v_h, preferred_element_type=F32)
            state = state_ref[h]
            cross = jnp.dot(q_h, state.astype(BF16), preferred_element_type=F32) * xi_ref[h]
            state_ref[h] = state_update(state, h, k_h, v_h)
            y = inner + cross
            mu = jnp.mean(y, axis=-1, keepdims=True)
            yc = y - mu
            yn = yc * lax.rsqrt(jnp.mean(yc * yc, axis=-1, keepdims=True) + EPS)
            gated = yn * gnw_ref[:, h * DV:(h + 1) * DV] * sg_ref[rows, h * DV:(h + 1) * DV].astype(F32)
            o_ref[rows, h * DV:(h + 1) * DV] = gated.astype(BF16)
        return carry

    lax.fori_loop(0, n_chunks, chunk, 0)


def _retention(qk, v, sg, qk_meta, v_meta, tables, gn_w, batch, seq, tt):
    chunk_decay, dmat, zeta, xi = tables
    tiles = seq // tt
    row = lambda b, t: b * tiles + t
    hv = RET_HEADS * RET_DV
    full = lambda shape: pl.BlockSpec(shape, lambda b, t: (0,) * len(shape))
    return pl.pallas_call(
        functools.partial(_retention_body, n_chunks=tt // RET_CHUNK),
        out_shape=jax.ShapeDtypeStruct((batch * seq, hv), BF16),
        grid=(batch, tiles),
        in_specs=[
            pl.BlockSpec(memory_space=pltpu.SMEM),
            pl.BlockSpec((tt, 1024), lambda b, t: (row(b, t), 0)),
            pl.BlockSpec((tt, 1024), lambda b, t: (row(b, t), 1)),
            pl.BlockSpec((tt, hv), lambda b, t: (row(b, t), 0)),
            pl.BlockSpec((tt, hv), lambda b, t: (row(b, t), 0)),
            pl.BlockSpec((RET_CHUNK, 1024), lambda b, t: (0, 1)),
            full((RET_CHUNK, hv)),
            full((RET_HEADS, RET_CHUNK, RET_CHUNK)),
            full((RET_HEADS, RET_CHUNK, 1)),
            full((RET_HEADS, RET_CHUNK, 1)),
            full((1, hv)),
        ],
        out_specs=pl.BlockSpec((tt, hv), lambda b, t: (row(b, t), 0)),
        scratch_shapes=[pltpu.VMEM((RET_HEADS, RET_DK, RET_DV), F32)],
        compiler_params=_params(2),
        name="retention",
    )(chunk_decay, qk, qk, v, sg, qk_meta, v_meta, dmat, zeta, xi, gn_w)


def _tail_body(x_ref, u_ref, gates_ref, yg_ref, um_ref, dww_ref, dwb_ref, lnw_ref, lnb_ref,
               wco_ref, bco_ref, wro_ref, wmo_ref, nfw_ref, wrt_ref, br_ref, tri_ref,
               h2_ref, xn_ref, topi_ref, gate_ref, rank_ref, cnt_ref,
               uext_ref, phase_ref, conv_ref, count_ref, *, tt, n_experts):
    b = pl.program_id(0)
    t = pl.program_id(1)
    C = RET_CHUNK
    d = x_ref.shape[1]

    @pl.when((b == 0) & (t == 0))
    def _zero_counts():
        count_ref[...] = jnp.zeros_like(count_ref)

    @pl.when(t == 0)
    def _seed_conv_history():
        uext_ref[0:CONV_HALO, :] = um_ref[C - CONV_HALO:C, :].astype(F32)

    uext_ref[CONV_HALO:CONV_HALO + tt, :] = u_ref[...].astype(F32)

    first = CONV_HALO - (CONV_WIDTH - 1)
    for s in range(d // LANES):
        cols = slice(s * LANES, (s + 1) * LANES)
        for phase in range(1, SUBLANES):
            phase_ref[phase - 1] = uext_ref[phase:phase + phase_ref.shape[1], cols]
        for c in range(tt // C):
            acc = jnp.zeros((C, LANES), F32)
            for k in range(CONV_WIDTH):
                a, phase = divmod(first + k, SUBLANES)
                r0 = c * C + a * SUBLANES
                window = uext_ref[r0:r0 + C, cols] if phase == 0 else phase_ref[phase - 1, r0:r0 + C, :]
                acc = acc + dww_ref[k:k + 1, cols] * window
            conv_ref[c * C:(c + 1) * C, cols] = acc + dwb_ref[:, cols]
    uext_ref[0:CONV_HALO, :] = uext_ref[tt:tt + CONV_HALO, :]

    cv = conv_ref[...]
    mu = jnp.mean(cv, axis=-1, keepdims=True)
    cc = cv - mu
    ln = cc * lax.rsqrt(jnp.mean(cc * cc, axis=-1, keepdims=True) + EPS) * lnw_ref[...] + lnb_ref[...]
    act = (ln * jax.nn.sigmoid(ln)).astype(BF16)
    y_conv = jnp.dot(act, wco_ref[...], preferred_element_type=F32) + bco_ref[...]
    y_ret = jnp.dot(yg_ref[...], wro_ref[...], preferred_element_type=F32)
    merged = (gates_ref[:, :d].astype(F32) * y_conv + gates_ref[:, d:].astype(F32) * y_ret).astype(BF16)
    h2 = x_ref[...] + jnp.dot(merged, wmo_ref[...], preferred_element_type=F32)
    h2_ref[...] = h2

    xn = h2 * lax.rsqrt(jnp.mean(h2 * h2, axis=-1, keepdims=True) + EPS) * nfw_ref[...]
    xn_ref[...] = xn

    logits = lax.dot_general(wrt_ref[...], xn, (((1,), (1,)), ((), ())),
                             precision=lax.Precision.HIGHEST, preferred_element_type=F32) + br_ref[...]
    eidx = lax.broadcasted_iota(I32, (n_experts, tt), 0)
    selected = jnp.zeros((n_experts, tt), F32)
    vals, idxs = [], []
    for _ in range(TOP_K):
        m = jnp.max(logits, axis=0, keepdims=True)
        idx = jnp.min(jnp.where(logits == m, eidx, n_experts), axis=0, keepdims=True)
        hit = eidx == idx
        vals.append(m)
        idxs.append(idx)
        selected = jnp.where(hit, 1.0, selected)
        logits = jnp.where(hit, -jnp.inf, logits)
    exps = [jnp.exp(v - vals[0]) for v in vals]
    denom = exps[0] + exps[1] + exps[2] + exps[3]

    before = jnp.dot(selected.astype(BF16), tri_ref[...], preferred_element_type=F32) + count_ref[:, 0:1]
    ranks = [jnp.sum(jnp.where(eidx == idx, before, 0.0), axis=0, keepdims=True).astype(I32) for idx in idxs]
    count_ref[...] = count_ref[...] + jnp.sum(selected, axis=1, keepdims=True)

    zi = jnp.zeros((8 - TOP_K, tt), I32)
    topi_ref[...] = jnp.concatenate(idxs + [zi], axis=0)
    rank_ref[...] = jnp.concatenate(ranks + [zi], axis=0)
    gate_ref[...] = jnp.concatenate([e / denom for e in exps] + [zi.astype(F32)], axis=0)
    cnt_ref[...] = count_ref[...]


def _tail(x2d, u, gates, yg, u_meta, p, batch, seq, tt, n_experts):
    d = x2d.shape[1]
    tiles = seq // tt
    rows = batch * seq
    row = lambda b, t: b * tiles + t
    full = lambda shape: pl.BlockSpec(shape, lambda b, t: (0,) * len(shape), pipeline_mode=pl.Buffered(1))
    tri = (jnp.arange(tt)[:, None] < jnp.arange(tt)[None, :]).astype(BF16)
    lane_rows = pl.BlockSpec((8, tt), lambda b, t: (0, row(b, t)))
    return pl.pallas_call(
        functools.partial(_tail_body, tt=tt, n_experts=n_experts),
        out_shape=(
            jax.ShapeDtypeStruct((rows, d), F32),
            jax.ShapeDtypeStruct((rows, d), F32),
            jax.ShapeDtypeStruct((8, rows), I32),
            jax.ShapeDtypeStruct((8, rows), F32),
            jax.ShapeDtypeStruct((8, rows), I32),
            jax.ShapeDtypeStruct((n_experts, LANES), F32),
        ),
        grid=(batch, tiles),
        in_specs=[
            pl.BlockSpec((tt, d), lambda b, t: (row(b, t), 0)),
            pl.BlockSpec((tt, 1024), lambda b, t: (row(b, t), 0)),
            pl.BlockSpec((tt, 2048), lambda b, t: (row(b, t), 0)),
            pl.BlockSpec((tt, yg.shape[1]), lambda b, t: (row(b, t), 0)),
            full((RET_CHUNK, 1024)),
            full((CONV_WIDTH, d)), full((1, d)), full((1, d)), full((1, d)),
            full((d, d)), full((1, d)),
            full((yg.shape[1], d)),
            full((d, d)),
            full((1, d)),
            full((n_experts, d)), full((n_experts, 1)),
            full((tt, tt)),
        ],
        out_specs=(
            pl.BlockSpec((tt, d), lambda b, t: (row(b, t), 0)),
            pl.BlockSpec((tt, d), lambda b, t: (row(b, t), 0)),
            lane_rows, lane_rows, lane_rows,
            pl.BlockSpec((n_experts, LANES), lambda b, t: (0, 0)),
        ),
        scratch_shapes=[
            pltpu.VMEM((CONV_HALO + tt, d), F32),
            pltpu.VMEM((SUBLANES - 1, CONV_HALO + tt - SUBLANES, LANES), F32),
            pltpu.VMEM((tt, d), F32),
            pltpu.VMEM((n_experts, LANES), F32),
        ],
        compiler_params=_params(2),
        name="mixer_tail",
    )(x2d, u, gates, yg, u_meta, p["dw_w"], p["dw_b"], p["ln_w"], p["ln_b"], p["w_conv_out"], p["b_conv_out"],
      p["w_ret_out"], p["w_mix_out"], p["norm_ffn_w"], p["w_router_t"], p["b_router"], tri)


def _dispatch_body(pend_ref, pcnt_ref, dest_ref, xn_ref, xs_hbm, zero_ref, sem, zsem, *, td, tm, n_experts):
    def zero_copy(e):
        start = pl.multiple_of(pend_ref[e] - tm, tm)
        return pltpu.make_async_copy(zero_ref, xs_hbm.at[pl.ds(start, tm)], zsem)

    @pl.when(pl.program_id(0) == 0)
    def _zero_last_tile_of_every_expert():
        zero_ref[...] = jnp.zeros_like(zero_ref)
        for e in range(n_experts):
            pl.when(pcnt_ref[e] > 0)(lambda e=e: zero_copy(e).start())
        for e in range(n_experts):
            pl.when(pcnt_ref[e] > 0)(lambda e=e: zero_copy(e).wait())

    def row_copy(r, dst_row):
        return pltpu.make_async_copy(xn_ref.at[pl.ds(r, 1)], xs_hbm.at[pl.ds(dst_row, 1)], sem)

    def issue(r, carry):
        for k in range(TOP_K):
            row_copy(r, dest_ref[k * td + r]).start(priority=k % DMA_PRIORITIES)
        return carry

    lax.fori_loop(0, td, issue, 0, unroll=8)

    def drain(r, carry):
        for k in range(TOP_K):
            row_copy(0, 0).wait()
        return carry

    lax.fori_loop(0, td, drain, 0, unroll=8)


def _dispatch(pends, pcounts, dest_tiles, xn, n_rows_out, td, tm, n_experts):
    rows, w = xn.shape
    return pl.pallas_call(
        functools.partial(_dispatch_body, td=td, tm=tm, n_experts=n_experts),
        out_shape=jax.ShapeDtypeStruct((n_rows_out, w), F32),
        grid_spec=pltpu.PrefetchScalarGridSpec(
            num_scalar_prefetch=2,
            grid=(rows // td,),
            in_specs=[
                pl.BlockSpec((TOP_K * td,), lambda i, *_: (i,), memory_space=pltpu.SMEM),
                pl.BlockSpec((td, w), lambda i, *_: (i, 0)),
            ],
            out_specs=pl.BlockSpec(memory_space=pl.ANY),
            scratch_shapes=[pltpu.VMEM((tm, w), F32), pltpu.SemaphoreType.DMA, pltpu.SemaphoreType.DMA],
        ),
        compiler_params=_params(1),
        name="moe_dispatch",
    )(pends, pcounts, dest_tiles, xn)


def _experts_body(tile_e_ref, n_used_ref, xs_ref, wgu_ref, bgu_ref, wd_ref, bd_ref, ys_ref, wgu_bf_ref, wd_bf_ref):
    p = pl.program_id(0)

    @pl.when(p < n_used_ref[0])
    def _():
        @pl.when((p == 0) | (tile_e_ref[p] != tile_e_ref[jnp.maximum(p - 1, 0)]))
        def _round_weights():
            wgu_bf_ref[...] = wgu_ref[0].astype(BF16)
            wd_bf_ref[...] = wd_ref[0].astype(BF16)

        x = xs_ref[...].astype(BF16)
        hgu = jnp.dot(x, wgu_bf_ref[...], preferred_element_type=F32) + bgu_ref[0]
        de = hgu.shape[1] // 2
        h_gate = jnp.minimum(hgu[:, :de], SWIGLU_LIMIT)
        h_up = jnp.clip(hgu[:, de:], -SWIGLU_LIMIT, SWIGLU_LIMIT)
        act = h_gate * jax.nn.sigmoid(SWIGLU_ALPHA * h_gate) * (h_up + 1.0)
        y = jnp.dot(act.astype(BF16), wd_bf_ref[...], preferred_element_type=F32) + bd_ref[0]
        ys_ref[...] = y


def _experts(tile_e, n_used, xs, wgu, bgu, wd, bd, tm):
    n_rows, w = xs.shape
    e, d, de2 = wgu.shape
    tile = lambda p, te, nu: jnp.minimum(p, nu[0] - 1)
    expert = lambda p, te, nu: te[tile(p, te, nu)]
    return pl.pallas_call(
        _experts_body,
        out_shape=jax.ShapeDtypeStruct((n_rows, w), F32),
        grid_spec=pltpu.PrefetchScalarGridSpec(
            num_scalar_prefetch=2,
            grid=(n_rows // tm,),
            in_specs=[
                pl.BlockSpec((tm, w), lambda p, te, nu: (tile(p, te, nu), 0)),
                pl.BlockSpec((1, d, de2), lambda p, te, nu: (expert(p, te, nu), 0, 0)),
                pl.BlockSpec((1, 1, de2), lambda p, te, nu: (expert(p, te, nu), 0, 0)),
                pl.BlockSpec((1, de2 // 2, d), lambda p, te, nu: (expert(p, te, nu), 0, 0)),
                pl.BlockSpec((1, 1, d), lambda p, te, nu: (expert(p, te, nu), 0, 0)),
            ],
            out_specs=pl.BlockSpec((tm, w), lambda p, te, nu: (tile(p, te, nu), 0)),
            scratch_shapes=[pltpu.VMEM((d, de2), BF16), pltpu.VMEM((de2 // 2, d), BF16)],
        ),
        compiler_params=_params(1),
        name="moe_experts",
    )(tile_e, n_used, xs, wgu, bgu, wd, bd)


def _combine_body(dest_ref, ys_hbm, h2_ref, gate_ref, nw_ref, o_ref, buf_ref, sem, *, tc):
    def row_copy(src_row, k, r):
        return pltpu.make_async_copy(ys_hbm.at[pl.ds(src_row, 1)], buf_ref.at[k, pl.ds(r, 1)], sem)

    def issue(r, carry):
        for k in range(TOP_K):
            row_copy(dest_ref[k * tc + r], k, r).start(priority=k % DMA_PRIORITIES)
        return carry

    lax.fori_loop(0, tc, issue, 0, unroll=8)

    g = gate_ref[...]
    g_cols = jnp.concatenate([g, jnp.zeros((LANES - g.shape[0], tc), F32)], axis=0).T

    def drain(r, carry):
        for k in range(TOP_K):
            row_copy(0, k, 0).wait()
        return carry

    lax.fori_loop(0, tc, drain, 0, unroll=8)

    h = h2_ref[...]
    for k in range(TOP_K):
        h = h + g_cols[:, k:k + 1] * buf_ref[k]
    o_ref[...] = h * lax.rsqrt(jnp.mean(h * h, axis=-1, keepdims=True) + EPS) * nw_ref[...]


def _combine(dest_tiles, ys, h2, gates, norm_w, tc):
    rows, d = h2.shape
    return pl.pallas_call(
        functools.partial(_combine_body, tc=tc),
        out_shape=jax.ShapeDtypeStruct((rows, d), F32),
        grid=(rows // tc,),
        in_specs=[
            pl.BlockSpec((TOP_K * tc,), lambda i: (i,), memory_space=pltpu.SMEM),
            pl.BlockSpec(memory_space=pl.ANY),
            pl.BlockSpec((tc, d), lambda i: (i, 0)),
            pl.BlockSpec((8, tc), lambda i: (0, i)),
            pl.BlockSpec((1, d), lambda i: (0, 0)),
        ],
        out_specs=pl.BlockSpec((tc, d), lambda i: (i, 0)),
        scratch_shapes=[pltpu.VMEM((TOP_K, tc, d), F32), pltpu.SemaphoreType.DMA],
        compiler_params=_params(1),
        name="moe_combine",
    )(dest_tiles, ys, h2, gates, norm_w)


def _rope_tables(positions):
    inv_freq = ROPE_BASE ** (-np.arange(0, RET_DK, 2, dtype=np.float64) / RET_DK)
    ang = np.asarray(positions, np.float64)[:, None] * inv_freq[None, :]
    return jnp.asarray(np.cos(ang), F32), jnp.asarray(np.sin(ang), F32)


def _retention_tables():
    c = RET_CHUNK
    log_gamma = jnp.log1p(-jnp.exp2(-5.0 - jnp.arange(RET_HEADS, dtype=F32)))
    idx = jnp.arange(c, dtype=F32)
    diff = idx[:, None] - idx[None, :]
    dmat = jnp.where(diff[None] >= 0, jnp.exp(log_gamma[:, None, None] * jnp.maximum(diff, 0.0)[None]), 0.0)
    zeta = jnp.exp(log_gamma[:, None] * (c - 1 - idx)[None, :])[:, :, None]
    xi = jnp.exp(log_gamma[:, None] * (idx + 1.0)[None, :])[:, :, None]
    chunk_decay = jnp.exp(log_gamma * c)
    return chunk_decay, dmat, zeta, xi


def _largest_tile(n, cap):
    t = cap
    while n % t:
        t //= 2
    return t


def _tile_major(a, tile):
    k, t = a.shape
    return a.reshape(k, t // tile, tile).transpose(1, 0, 2).reshape(-1)


def kernel(x, meta_tokens, norm_mix_w, w_in, conv_dw_w, conv_dw_b, conv_ln_w, conv_ln_b, w_conv_out,
           b_conv_out, ret_gn_w, w_ret_out, w_mix_out, norm_ffn_w, w_router, b_router, w_gate_up,
           b_gate_up, w_down, b_down, norm_final_w):
    batch, seq, d = x.shape
    depth = w_in.shape[0]
    n_experts = w_router.shape[-1]
    assert depth == 1 and d == 1024 and seq % RET_CHUNK == 0
    assert w_in.shape[-1] == 10 * 1024 and w_gate_up.shape[-1] == 2 * d
    rows = batch * seq

    tm_in = _largest_tile(seq, 1024)
    tt_ret = _largest_tile(seq, 512)
    tt_tail = _largest_tile(seq, 512)
    td = _largest_tile(rows, 512)
    tc = _largest_tile(rows, 512)
    tm_e = 512

    x2d = x.reshape(rows, d)
    w_in_bf = w_in[0].astype(BF16)
    row1 = lambda v: v.reshape(1, -1)
    norm_w = row1(norm_mix_w[0])

    cos, sin = _rope_tables(N_META + np.arange(seq))
    u, hn = _inproj_glu(x2d, norm_w, w_in_bf, tm_in)
    qk = _inproj_rotary(hn, w_in_bf, cos, sin, tm_in)
    v = _inproj_elementwise(hn, w_in_bf, WB_V, lambda z: z, tm_in, "inproj_value")
    sg = _inproj_elementwise(hn, w_in_bf, WB_G, _silu, tm_in, "inproj_silu")
    gates = _inproj_elementwise(hn, w_in_bf, WB_GATE, jax.nn.sigmoid, tm_in, "inproj_gate")

    pad = RET_CHUNK - N_META
    meta_chunk = jnp.concatenate([jnp.zeros((pad, d), x.dtype), meta_tokens.astype(x.dtype)], axis=0)
    cos_m, sin_m = _rope_tables(np.maximum(np.arange(RET_CHUNK) - pad, 0))
    u_meta, hn_meta = _inproj_glu(meta_chunk, norm_w, w_in_bf, RET_CHUNK)
    qk_meta = _inproj_rotary(hn_meta, w_in_bf, cos_m, sin_m, RET_CHUNK)
    v_meta = _inproj_elementwise(hn_meta, w_in_bf, WB_V, lambda z: z, RET_CHUNK, "inproj_value_meta")

    yg = _retention(qk, v, sg, qk_meta, v_meta, _retention_tables(), row1(ret_gn_w[0]), batch, seq, tt_ret)

    tail_params = dict(
        dw_w=conv_dw_w[0], dw_b=row1(conv_dw_b[0]), ln_w=row1(conv_ln_w[0]), ln_b=row1(conv_ln_b[0]),
        w_conv_out=w_conv_out[0].astype(BF16), b_conv_out=row1(b_conv_out[0]),
        w_ret_out=w_ret_out[0].astype(BF16), w_mix_out=w_mix_out[0].astype(BF16),
        norm_ffn_w=row1(norm_ffn_w[0]), w_router_t=w_router[0].T, b_router=b_router[0].reshape(-1, 1))
    h2, xn, topi, gate_w, rank, counts = _tail(x2d, u, gates, yg, u_meta, tail_params, batch, seq, tt_tail,
                                               n_experts)

    counts = counts[:, 0].astype(I32)
    pcounts = (counts + tm_e - 1) // tm_e * tm_e
    pends = jnp.cumsum(pcounts).astype(I32)
    pstarts = pends - pcounts
    experts = jnp.arange(n_experts, dtype=I32)
    start_of = jnp.sum(jnp.where(topi[:TOP_K, :, None] == experts, pstarts, 0), axis=-1)
    dest = start_of + rank[:TOP_K]
    n_tiles = rows * TOP_K // tm_e + n_experts
    tile_starts = jnp.arange(n_tiles, dtype=I32) * tm_e
    tile_e = jnp.minimum(jnp.sum(pends[None, :] <= tile_starts[:, None], axis=-1), n_experts - 1).astype(I32)
    n_used = (pends[-1:] // tm_e).astype(I32)

    xs = _dispatch(pends, pcounts, _tile_major(dest, td), xn, n_tiles * tm_e, td, tm_e, n_experts)
    ys = _experts(tile_e, n_used, xs, w_gate_up[0], b_gate_up[0][:, None, :],
                  w_down[0], b_down[0][:, None, :], tm_e)
    out = _combine(_tile_major(dest, tc), ys, h2, gate_w, row1(norm_final_w), tc)
    return out.reshape(batch, seq, d)
```

```python
import functools

import jax
import jax.numpy as jnp
import numpy as np
from jax import lax
from jax.experimental import pallas as pl
from jax.experimental.pallas import tpu as pltpu

N_META = 16
CONV_WIDTH = 31
RET_HEADS = 4
RET_DK = 256
RET_DV = 512
RET_CHUNK = 128
ROPE_BASE = 10000.0
TOP_K = 4
SWIGLU_LIMIT = 7.0
SWIGLU_ALPHA = 1.702
EPS = 1e-5

LANES = 128
SUBLANES = 8
CONV_HALO = 32
DMA_PRIORITIES = 2
VMEM_LIMIT = 56 * 1024 * 1024

F32 = jnp.float32
BF16 = jnp.bfloat16
I32 = jnp.int32

WB_CONV_A, WB_CONV_B, WB_Q, WB_V, WB_G, WB_GATE = 0, 1, 2, 4, 6, 8


def _params(n_grid_axes):
    return pltpu.CompilerParams(dimension_semantics=("arbitrary",) * n_grid_axes,
                                vmem_limit_bytes=VMEM_LIMIT)


def _inproj_glu_body(x_ref, nw_ref, wa_ref, wb_ref, u_ref, hn_ref):
    x = x_ref[...]
    ms = jnp.mean(x * x, axis=-1, keepdims=True)
    hn = (x * lax.rsqrt(ms + EPS) * nw_ref[...]).astype(BF16)
    hn_ref[...] = hn
    a = jnp.dot(hn, wa_ref[...], preferred_element_type=F32)
    b = jnp.dot(hn, wb_ref[...], preferred_element_type=F32)
    u_ref[...] = (a * jax.nn.sigmoid(b)).astype(BF16)


def _inproj_glu(x2d, norm_w, w_in_bf, tm):
    rows, d = x2d.shape
    return pl.pallas_call(
        _inproj_glu_body,
        out_shape=(jax.ShapeDtypeStruct((rows, 1024), BF16), jax.ShapeDtypeStruct((rows, d), BF16)),
        grid=(rows // tm,),
        in_specs=[
            pl.BlockSpec((tm, d), lambda i: (i, 0)),
            pl.BlockSpec((1, d), lambda i: (0, 0)),
            pl.BlockSpec((d, 1024), lambda i: (0, WB_CONV_A)),
            pl.BlockSpec((d, 1024), lambda i: (0, WB_CONV_B)),
        ],
        out_specs=(pl.BlockSpec((tm, 1024), lambda i: (i, 0)), pl.BlockSpec((tm, d), lambda i: (i, 0))),
        compiler_params=_params(1),
        name="inproj_glu",
    )(x2d, norm_w, w_in_bf, w_in_bf)


def _inproj_rotary_body(hn_ref, w_ref, cos_ref, sin_ref, o_ref):
    acc = jnp.dot(hn_ref[...], w_ref[...], preferred_element_type=F32)
    scale = jnp.where(pl.program_id(1) == 0, RET_DK ** -0.5, 1.0).astype(F32)
    cos = cos_ref[...] * scale
    sin = sin_ref[...] * scale
    half = RET_DK // 2
    for h in range(RET_HEADS):
        x1 = acc[:, h * RET_DK:h * RET_DK + half]
        x2 = acc[:, h * RET_DK + half:(h + 1) * RET_DK]
        o_ref[:, h * RET_DK:h * RET_DK + half] = (x1 * cos - x2 * sin).astype(BF16)
        o_ref[:, h * RET_DK + half:(h + 1) * RET_DK] = (x2 * cos + x1 * sin).astype(BF16)


def _inproj_rotary(hn, w_in_bf, cos, sin, tm):
    rows, d = hn.shape
    period_tiles = cos.shape[0] // tm
    return pl.pallas_call(
        _inproj_rotary_body,
        out_shape=jax.ShapeDtypeStruct((rows, 2048), BF16),
        grid=(rows // tm, 2),
        in_specs=[
            pl.BlockSpec((tm, d), lambda i, j: (i, 0)),
            pl.BlockSpec((d, 1024), lambda i, j: (0, WB_Q + j)),
            pl.BlockSpec((tm, LANES), lambda i, j: (i % period_tiles, 0)),
            pl.BlockSpec((tm, LANES), lambda i, j: (i % period_tiles, 0)),
        ],
        out_specs=pl.BlockSpec((tm, 1024), lambda i, j: (i, j)),
        compiler_params=_params(2),
        name="inproj_rotary",
    )(hn, w_in_bf, cos, sin)


def _inproj_elementwise_body(hn_ref, w_ref, o_ref, *, fn):
    o_ref[...] = fn(jnp.dot(hn_ref[...], w_ref[...], preferred_element_type=F32)).astype(BF16)


def _silu(x):
    return x * jax.nn.sigmoid(x)


def _inproj_elementwise(hn, w_in_bf, first_block, fn, tm, name):
    rows, d = hn.shape
    return pl.pallas_call(
        functools.partial(_inproj_elementwise_body, fn=fn),
        out_shape=jax.ShapeDtypeStruct((rows, 2048), BF16),
        grid=(rows // tm, 2),
        in_specs=[
            pl.BlockSpec((tm, d), lambda i, j: (i, 0)),
            pl.BlockSpec((d, 1024), lambda i, j: (0, first_block + j)),
        ],
        out_specs=pl.BlockSpec((tm, 1024), lambda i, j: (i, j)),
        compiler_params=_params(2),
        name=name,
    )(hn, w_in_bf)


def _dot_t0(a, b):
    return lax.dot_general(a, b, (((0,), (0,)), ((), ())), preferred_element_type=F32)


def _dot_t1(a, b):
    return lax.dot_general(a, b, (((1,), (1,)), ((), ())), preferred_element_type=F32)


def _retention_body(cd_ref, q_ref, k_ref, v_ref, sg_ref, km_ref, vm_ref, dmat_ref, zeta_ref, xi_ref, gnw_ref,
                    o_ref, state_ref, *, n_chunks):
    t = pl.program_id(1)
    C, DK, DV = RET_CHUNK, RET_DK, RET_DV

    def state_update(prev, h, k_h, v_h):
        kz = (k_h.astype(F32) * zeta_ref[h]).astype(BF16)
        return prev * cd_ref[h] + _dot_t0(kz, v_h)

    @pl.when(t == 0)
    def _seed_from_meta_chunk():
        for h in range(RET_HEADS):
            state_ref[h] = state_update(jnp.zeros((DK, DV), F32), h,
                                        km_ref[:, h * DK:(h + 1) * DK], vm_ref[:, h * DV:(h + 1) * DV])

    def chunk(c, carry):
        rows = pl.ds(pl.multiple_of(c * C, C), C)
        for h in range(RET_HEADS):
            q_h = q_ref[rows, h * DK:(h + 1) * DK]
            k_h = k_ref[rows, h * DK:(h + 1) * DK]
            v_h = v_ref[rows, h * DV:(h + 1) * DV]
            scores = _dot_t1(q_h, k_h) * dmat_ref[h]
            inner = jnp.dot(scores.astype(BF16), v_h, preferred_element_type=F32)
            state = state_ref[h]
            cross = jnp.dot(q_h, state.astype(BF16), preferred_element_type=F32) * xi_ref[h]
            state_ref[h] = state_update(state, h, k_h, v_h)
            y = inner + cross
            mu = jnp.mean(y, axis=-1, keepdims=True)
            yc = y - mu
            yn = yc * lax.rsqrt(jnp.mean(yc * yc, axis=-1, keepdims=True) + EPS)
            gated = yn * gnw_ref[:, h * DV:(h + 1) * DV] * sg_ref[rows, h * DV:(h + 1) * DV].astype(F32)
            o_ref[rows, h * DV:(h + 1) * DV] = gated.astype(BF16)
        return carry

    lax.fori_loop(0, n_chunks, chunk, 0)


def _retention(qk, v, sg, qk_meta, v_meta, tables, gn_w, batch, seq, tt):
    chunk_decay, dmat, zeta, xi = tables
    tiles = seq // tt
    row = lambda b, t: b * tiles + t
    hv = RET_HEADS * RET_DV
    full = lambda shape: pl.BlockSpec(shape, lambda b, t: (0,) * len(shape))
    return pl.pallas_call(
        functools.partial(_retention_body, n_chunks=tt // RET_CHUNK),
        out_shape=jax.ShapeDtypeStruct((batch * seq, hv), BF16),
        grid=(batch, tiles),
        in_specs=[
            pl.BlockSpec(memory_space=pltpu.SMEM),
            pl.BlockSpec((tt, 1024), lambda b, t: (row(b, t), 0)),
            pl.BlockSpec((tt, 1024), lambda b, t: (row(b, t), 1)),
            pl.BlockSpec((tt, hv), lambda b, t: (row(b, t), 0)),
            pl.BlockSpec((tt, hv), lambda b, t: (row(b, t), 0)),
            pl.BlockSpec((RET_CHUNK, 1024), lambda b, t: (0, 1)),
            full((RET_CHUNK, hv)),
            full((RET_HEADS, RET_CHUNK, RET_CHUNK)),
            full((RET_HEADS, RET_CHUNK, 1)),
            full((RET_HEADS, RET_CHUNK, 1)),
            full((1, hv)),
        ],
        out_specs=pl.BlockSpec((tt, hv), lambda b, t: (row(b, t), 0)),
        scratch_shapes=[pltpu.VMEM((RET_HEADS, RET_DK, RET_DV), F32)],
        compiler_params=_params(2),
        name="retention",
    )(chunk_decay, qk, qk, v, sg, qk_meta, v_meta, dmat, zeta, xi, gn_w)


def _tail_body(x_ref, u_ref, gates_ref, yg_ref, um_ref, dww_ref, dwb_ref, lnw_ref, lnb_ref,
               wco_ref, bco_ref, wro_ref, wmo_ref, nfw_ref, wrt_ref, br_ref, tri_ref,
               h2_ref, xn_ref, topi_ref, gate_ref, rank_ref, cnt_ref,
               uext_ref, phase_ref, conv_ref, count_ref, *, tt, n_experts):
    b = pl.program_id(0)
    t = pl.program_id(1)
    C = RET_CHUNK
    d = x_ref.shape[1]

    @pl.when((b == 0) & (t == 0))
    def _zero_counts():
        count_ref[...] = jnp.zeros_like(count_ref)

    @pl.when(t == 0)
    def _seed_conv_history():
        uext_ref[0:CONV_HALO, :] = um_ref[C - CONV_HALO:C, :].astype(F32)

    uext_ref[CONV_HALO:CONV_HALO + tt, :] = u_ref[...].astype(F32)

    first = CONV_HALO - (CONV_WIDTH - 1)
    for s in range(d // LANES):
        cols = slice(s * LANES, (s + 1) * LANES)
        for phase in range(1, SUBLANES):
            phase_ref[phase - 1] = uext_ref[phase:phase + phase_ref.shape[1], cols]
        for c in range(tt // C):
            acc = jnp.zeros((C, LANES), F32)
            for k in range(CONV_WIDTH):
                a, phase = divmod(first + k, SUBLANES)
                r0 = c * C + a * SUBLANES
                window = uext_ref[r0:r0 + C, cols] if phase == 0 else phase_ref[phase - 1, r0:r0 + C, :]
                acc = acc + dww_ref[k:k + 1, cols] * window
            conv_ref[c * C:(c + 1) * C, cols] = acc + dwb_ref[:, cols]
    uext_ref[0:CONV_HALO, :] = uext_ref[tt:tt + CONV_HALO, :]

    cv = conv_ref[...]
    mu = jnp.mean(cv, axis=-1, keepdims=True)
    cc = cv - mu
    ln = cc * lax.rsqrt(jnp.mean(cc * cc, axis=-1, keepdims=True) + EPS) * lnw_ref[...] + lnb_ref[...]
    act = (ln * jax.nn.sigmoid(ln)).astype(BF16)
    y_conv = jnp.dot(act, wco_ref[...], preferred_element_type=F32) + bco_ref[...]
    y_ret = jnp.dot(yg_ref[...], wro_ref[...], preferred_element_type=F32)
    merged = (gates_ref[:, :d].astype(F32) * y_conv + gates_ref[:, d:].astype(F32) * y_ret).astype(BF16)
    h2 = x_ref[...] + jnp.dot(merged, wmo_ref[...], preferred_element_type=F32)
    h2_ref[...] = h2

    xn = h2 * lax.rsqrt(jnp.mean(h2 * h2, axis=-1, keepdims=True) + EPS) * nfw_ref[...]
    xn_ref[...] = xn

    logits = lax.dot_general(wrt_ref[...], xn, (((1,), (1,)), ((), ())),
                             precision=lax.Precision.HIGHEST, preferred_element_type=F32) + br_ref[...]
    eidx = lax.broadcasted_iota(I32, (n_experts, tt), 0)
    selected = jnp.zeros((n_experts, tt), F32)
    vals, idxs = [], []
    for _ in range(TOP_K):
        m = jnp.max(logits, axis=0, keepdims=True)
        idx = jnp.min(jnp.where(logits == m, eidx, n_experts), axis=0, keepdims=True)
        hit = eidx == idx
        vals.append(m)
        idxs.append(idx)
        selected = jnp.where(hit, 1.0, selected)
        logits = jnp.where(hit, -jnp.inf, logits)
    exps = [jnp.exp(v - vals[0]) for v in vals]
    denom = exps[0] + exps[1] + exps[2] + exps[3]

    before = jnp.dot(selected.astype(BF16), tri_ref[...], preferred_element_type=F32) + count_ref[:, 0:1]
    ranks = [jnp.sum(jnp.where(eidx == idx, before, 0.0), axis=0, keepdims=True).astype(I32) for idx in idxs]
    count_ref[...] = count_ref[...] + jnp.sum(selected, axis=1, keepdims=True)

    zi = jnp.zeros((8 - TOP_K, tt), I32)
    topi_ref[...] = jnp.concatenate(idxs + [zi], axis=0)
    rank_ref[...] = jnp.concatenate(ranks + [zi], axis=0)
    gate_ref[...] = jnp.concatenate([e / denom for e in exps] + [zi.astype(F32)], axis=0)
    cnt_ref[...] = count_ref[...]


def _tail(x2d, u, gates, yg, u_meta, p, batch, seq, tt, n_experts):
    d = x2d.shape[1]
    tiles = seq // tt
    rows = batch * seq
    row = lambda b, t: b * tiles + t
    full = lambda shape: pl.BlockSpec(shape, lambda b, t: (0,) * len(shape), pipeline_mode=pl.Buffered(1))
    tri = (jnp.arange(tt)[:, None] < jnp.arange(tt)[None, :]).astype(BF16)
    lane_rows = pl.BlockSpec((8, tt), lambda b, t: (0, row(b, t)))
    return pl.pallas_call(
        functools.partial(_tail_body, tt=tt, n_experts=n_experts),
        out_shape=(
            jax.ShapeDtypeStruct((rows, d), F32),
            jax.ShapeDtypeStruct((rows, d), F32),
            jax.ShapeDtypeStruct((8, rows), I32),
            jax.ShapeDtypeStruct((8, rows), F32),
            jax.ShapeDtypeStruct((8, rows), I32),
            jax.ShapeDtypeStruct((n_experts, LANES), F32),
        ),
        grid=(batch, tiles),
        in_specs=[
            pl.BlockSpec((tt, d), lambda b, t: (row(b, t), 0)),
            pl.BlockSpec((tt, 1024), lambda b, t: (row(b, t), 0)),
            pl.BlockSpec((tt, 2048), lambda b, t: (row(b, t), 0)),
            pl.BlockSpec((tt, yg.shape[1]), lambda b, t: (row(b, t), 0)),
            full((RET_CHUNK, 1024)),
            full((CONV_WIDTH, d)), full((1, d)), full((1, d)), full((1, d)),
            full((d, d)), full((1, d)),
            full((yg.shape[1], d)),
            full((d, d)),
            full((1, d)),
            full((n_experts, d)), full((n_experts, 1)),
            full((tt, tt)),
        ],
        out_specs=(
            pl.BlockSpec((tt, d), lambda b, t: (row(b, t), 0)),
            pl.BlockSpec((tt, d), lambda b, t: (row(b, t), 0)),
            lane_rows, lane_rows, lane_rows,
            pl.BlockSpec((n_experts, LANES), lambda b, t: (0, 0)),
        ),
        scratch_shapes=[
            pltpu.VMEM((CONV_HALO + tt, d), F32),
            pltpu.VMEM((SUBLANES - 1, CONV_HALO + tt - SUBLANES, LANES), F32),
            pltpu.VMEM((tt, d), F32),
            pltpu.VMEM((n_experts, LANES), F32),
        ],
        compiler_params=_params(2),
        name="mixer_tail",
    )(x2d, u, gates, yg, u_meta, p["dw_w"], p["dw_b"], p["ln_w"], p["ln_b"], p["w_conv_out"], p["b_conv_out"],
      p["w_ret_out"], p["w_mix_out"], p["norm_ffn_w"], p["w_router_t"], p["b_router"], tri)


def _dispatch_body(pend_ref, pcnt_ref, dest_ref, xn_ref, xs_hbm, zero_ref, sem, zsem, *, td, tm, n_experts):
    def zero_copy(e):
        start = pl.multiple_of(pend_ref[e] - tm, tm)
        return pltpu.make_async_copy(zero_ref, xs_hbm.at[pl.ds(start, tm)], zsem)

    @pl.when(pl.program_id(0) == 0)
    def _zero_last_tile_of_every_expert():
        zero_ref[...] = jnp.zeros_like(zero_ref)
        for e in range(n_experts):
            pl.when(pcnt_ref[e] > 0)(lambda e=e: zero_copy(e).start())
        for e in range(n_experts):
            pl.when(pcnt_ref[e] > 0)(lambda e=e: zero_copy(e).wait())

    def row_copy(r, dst_row):
        return pltpu.make_async_copy(xn_ref.at[pl.ds(r, 1)], xs_hbm.at[pl.ds(dst_row, 1)], sem)

    for r in range(td):
        for k in range(TOP_K):
            row_copy(r, dest_ref[k * td + r]).start(priority=k % DMA_PRIORITIES)

    def drain(r, carry):
        for k in range(TOP_K):
            row_copy(0, 0).wait()
        return carry

    lax.fori_loop(0, td, drain, 0, unroll=8)


def _dispatch(pends, pcounts, dest_tiles, xn, n_rows_out, td, tm, n_experts):
    rows, w = xn.shape
    return pl.pallas_call(
        functools.partial(_dispatch_body, td=td, tm=tm, n_experts=n_experts),
        out_shape=jax.ShapeDtypeStruct((n_rows_out, w), F32),
        grid_spec=pltpu.PrefetchScalarGridSpec(
            num_scalar_prefetch=2,
            grid=(rows // td,),
            in_specs=[
                pl.BlockSpec((TOP_K * td,), lambda i, *_: (i,), memory_space=pltpu.SMEM),
                pl.BlockSpec((td, w), lambda i, *_: (i, 0)),
            ],
            out_specs=pl.BlockSpec(memory_space=pl.ANY),
            scratch_shapes=[pltpu.VMEM((tm, w), F32), pltpu.SemaphoreType.DMA, pltpu.SemaphoreType.DMA],
        ),
        compiler_params=_params(1),
        name="moe_dispatch",
    )(pends, pcounts, dest_tiles, xn)


def _experts_body(tile_e_ref, n_used_ref, ptiles_ref, xs_ref, wgu_hbm, bgu_ref, wd_hbm, bd_ref, ys_ref,
                  wgu_stage_ref, wd_stage_ref, wgu_bf_ref, wd_bf_ref, sems):
    p = pl.program_id(0)
    n_used = n_used_ref[0]

    def weight_copies(e):
        return (pltpu.make_async_copy(wgu_hbm.at[e], wgu_stage_ref, sems.at[0]),
                pltpu.make_async_copy(wd_hbm.at[e], wd_stage_ref, sems.at[1]))

    @pl.when(p < n_used)
    def _():
        e = tile_e_ref[p]

        @pl.when(p == 0)
        def _fetch_first_expert():
            for c in weight_copies(e):
                c.start()

        @pl.when((p == 0) | (e != tile_e_ref[jnp.maximum(p - 1, 0)]))
        def _switch_expert():
            for c in weight_copies(e):
                c.wait()
            wgu_bf_ref[...] = wgu_stage_ref[...].astype(BF16)
            wd_bf_ref[...] = wd_stage_ref[...].astype(BF16)
            nxt = p + ptiles_ref[e]

            @pl.when(nxt < n_used)
            def _fetch_next_expert():
                for c in weight_copies(tile_e_ref[nxt]):
                    c.start()

        x = xs_ref[...].astype(BF16)
        hgu = jnp.dot(x, wgu_bf_ref[...], preferred_element_type=F32) + bgu_ref[0]
        de = hgu.shape[1] // 2
        h_gate = jnp.minimum(hgu[:, :de], SWIGLU_LIMIT)
        h_up = jnp.clip(hgu[:, de:], -SWIGLU_LIMIT, SWIGLU_LIMIT)
        act = h_gate * jax.nn.sigmoid(SWIGLU_ALPHA * h_gate) * (h_up + 1.0)
        y = jnp.dot(act.astype(BF16), wd_bf_ref[...], preferred_element_type=F32) + bd_ref[0]
        ys_ref[...] = y


def _experts(tile_e, n_used, ptiles, xs, wgu, bgu, wd, bd, tm):
    n_rows, w = xs.shape
    e, d, de2 = wgu.shape
    tile = lambda p, te, nu, pt: jnp.minimum(p, nu[0] - 1)
    expert = lambda p, te, nu, pt: te[tile(p, te, nu, pt)]
    return pl.pallas_call(
        _experts_body,
        out_shape=jax.ShapeDtypeStruct((n_rows, w), F32),
        grid_spec=pltpu.PrefetchScalarGridSpec(
            num_scalar_prefetch=3,
            grid=(n_rows // tm,),
            in_specs=[
                pl.BlockSpec((tm, w), lambda p, te, nu, pt: (tile(p, te, nu, pt), 0)),
                pl.BlockSpec(memory_space=pl.ANY),
                pl.BlockSpec((1, 1, de2), lambda p, te, nu, pt: (expert(p, te, nu, pt), 0, 0)),
                pl.BlockSpec(memory_space=pl.ANY),
                pl.BlockSpec((1, 1, d), lambda p, te, nu, pt: (expert(p, te, nu, pt), 0, 0)),
            ],
            out_specs=pl.BlockSpec((tm, w), lambda p, te, nu, pt: (tile(p, te, nu, pt), 0)),
            scratch_shapes=[pltpu.VMEM((d, de2), F32), pltpu.VMEM((de2 // 2, d), F32),
                            pltpu.VMEM((d, de2), BF16), pltpu.VMEM((de2 // 2, d), BF16),
                            pltpu.SemaphoreType.DMA((2,))],
        ),
        compiler_params=_params(1),
        name="moe_experts",
    )(tile_e, n_used, ptiles, xs, wgu, bgu, wd, bd)


def _combine_body(dest_ref, dest_next_ref, ys_hbm, h2_ref, gate_ref, nw_ref, o_ref, buf_ref, sems, *, th):
    i = pl.program_id(0)
    half_words = TOP_K * th

    def row_copy(src_row, slot, k, r):
        return pltpu.make_async_copy(ys_hbm.at[pl.ds(src_row, 1)], buf_ref.at[slot, k, pl.ds(r, 1)],
                                     sems.at[slot])

    def issue(idx_ref, base, slot):
        for r in range(th):
            for k in range(TOP_K):
                row_copy(idx_ref[base + k * th + r], slot, k, r).start(priority=k % DMA_PRIORITIES)

    def drain(slot):
        def body(r, carry):
            for k in range(TOP_K):
                row_copy(0, slot, k, 0).wait()
            return carry
        lax.fori_loop(0, th, body, 0, unroll=8)

    g = gate_ref[...]
    g_cols = jnp.concatenate([g, jnp.zeros((LANES - g.shape[0], 2 * th), F32)], axis=0).T

    def finish(half):
        rows = slice(half * th, (half + 1) * th)
        h = h2_ref[rows, :]
        for k in range(TOP_K):
            h = h + g_cols[rows, k:k + 1] * buf_ref[half, k]
        o_ref[rows, :] = h * lax.rsqrt(jnp.mean(h * h, axis=-1, keepdims=True) + EPS) * nw_ref[...]

    @pl.when(i == 0)
    def _first_half_tile():
        def body(r, carry):
            for k in range(TOP_K):
                row_copy(dest_ref[k * th + r], 0, k, r).start(priority=k % DMA_PRIORITIES)
            return carry
        lax.fori_loop(0, th, body, 0, unroll=8)

    issue(dest_ref, half_words, 1)
    drain(0)
    finish(0)

    @pl.when(i + 1 < pl.num_programs(0))
    def _next_step_first_half_tile():
        issue(dest_next_ref, 0, 0)

    drain(1)
    finish(1)


def _combine(dest_tiles, ys, h2, gates, norm_w, th):
    rows, d = h2.shape
    steps = rows // (2 * th)
    return pl.pallas_call(
        functools.partial(_combine_body, th=th),
        out_shape=jax.ShapeDtypeStruct((rows, d), F32),
        grid=(steps,),
        in_specs=[
            pl.BlockSpec((2 * TOP_K * th,), lambda i: (i,), memory_space=pltpu.SMEM),
            pl.BlockSpec((TOP_K * th,), lambda i: (jnp.minimum(2 * i + 2, 2 * steps - 1),),
                         memory_space=pltpu.SMEM),
            pl.BlockSpec(memory_space=pl.ANY),
            pl.BlockSpec((2 * th, d), lambda i: (i, 0)),
            pl.BlockSpec((8, 2 * th), lambda i: (0, i)),
            pl.BlockSpec((1, d), lambda i: (0, 0)),
        ],
        out_specs=pl.BlockSpec((2 * th, d), lambda i: (i, 0)),
        scratch_shapes=[pltpu.VMEM((2, TOP_K, th, d), F32), pltpu.SemaphoreType.DMA((2,))],
        compiler_params=_params(1),
        name="moe_combine",
    )(dest_tiles, dest_tiles, ys, h2, gates, norm_w)


def _rope_tables(positions):
    inv_freq = ROPE_BASE ** (-np.arange(0, RET_DK, 2, dtype=np.float64) / RET_DK)
    ang = np.asarray(positions, np.float64)[:, None] * inv_freq[None, :]
    return jnp.asarray(np.cos(ang), F32), jnp.asarray(np.sin(ang), F32)


def _retention_tables():
    c = RET_CHUNK
    log_gamma = jnp.log1p(-jnp.exp2(-5.0 - jnp.arange(RET_HEADS, dtype=F32)))
    idx = jnp.arange(c, dtype=F32)
    diff = idx[:, None] - idx[None, :]
    dmat = jnp.where(diff[None] >= 0, jnp.exp(log_gamma[:, None, None] * jnp.maximum(diff, 0.0)[None]), 0.0)
    zeta = jnp.exp(log_gamma[:, None] * (c - 1 - idx)[None, :])[:, :, None]
    xi = jnp.exp(log_gamma[:, None] * (idx + 1.0)[None, :])[:, :, None]
    chunk_decay = jnp.exp(log_gamma * c)
    return chunk_decay, dmat, zeta, xi


def _largest_tile(n, cap):
    t = cap
    while n % t:
        t //= 2
    return t


def _tile_major(a, tile):
    k, t = a.shape
    return a.reshape(k, t // tile, tile).transpose(1, 0, 2).reshape(-1)


def kernel(x, meta_tokens, norm_mix_w, w_in, conv_dw_w, conv_dw_b, conv_ln_w, conv_ln_b, w_conv_out,
           b_conv_out, ret_gn_w, w_ret_out, w_mix_out, norm_ffn_w, w_router, b_router, w_gate_up,
           b_gate_up, w_down, b_down, norm_final_w):
    batch, seq, d = x.shape
    depth = w_in.shape[0]
    n_experts = w_router.shape[-1]
    assert depth == 1 and d == 1024 and seq % RET_CHUNK == 0
    assert w_in.shape[-1] == 10 * 1024 and w_gate_up.shape[-1] == 2 * d
    rows = batch * seq

    tm_in = _largest_tile(seq, 1024)
    tt_ret = _largest_tile(seq, 512)
    tt_tail = _largest_tile(seq, 512)
    td = _largest_tile(rows, 512)
    tc_half = _largest_tile(rows, 512) // 2
    tm_e = 512

    x2d = x.reshape(rows, d)
    w_in_bf = w_in[0].astype(BF16)
    row1 = lambda v: v.reshape(1, -1)
    norm_w = row1(norm_mix_w[0])

    cos, sin = _rope_tables(N_META + np.arange(seq))
    u, hn = _inproj_glu(x2d, norm_w, w_in_bf, tm_in)
    qk = _inproj_rotary(hn, w_in_bf, cos, sin, tm_in)
    v = _inproj_elementwise(hn, w_in_bf, WB_V, lambda z: z, tm_in, "inproj_value")
    sg = _inproj_elementwise(hn, w_in_bf, WB_G, _silu, tm_in, "inproj_silu")
    gates = _inproj_elementwise(hn, w_in_bf, WB_GATE, jax.nn.sigmoid, tm_in, "inproj_gate")

    pad = RET_CHUNK - N_META
    meta_chunk = jnp.concatenate([jnp.zeros((pad, d), x.dtype), meta_tokens.astype(x.dtype)], axis=0)
    cos_m, sin_m = _rope_tables(np.maximum(np.arange(RET_CHUNK) - pad, 0))
    u_meta, hn_meta = _inproj_glu(meta_chunk, norm_w, w_in_bf, RET_CHUNK)
    qk_meta = _inproj_rotary(hn_meta, w_in_bf, cos_m, sin_m, RET_CHUNK)
    v_meta = _inproj_elementwise(hn_meta, w_in_bf, WB_V, lambda z: z, RET_CHUNK, "inproj_value_meta")

    yg = _retention(qk, v, sg, qk_meta, v_meta, _retention_tables(), row1(ret_gn_w[0]), batch, seq, tt_ret)

    tail_params = dict(
        dw_w=conv_dw_w[0], dw_b=row1(conv_dw_b[0]), ln_w=row1(conv_ln_w[0]), ln_b=row1(conv_ln_b[0]),
        w_conv_out=w_conv_out[0].astype(BF16), b_conv_out=row1(b_conv_out[0]),
        w_ret_out=w_ret_out[0].astype(BF16), w_mix_out=w_mix_out[0].astype(BF16),
        norm_ffn_w=row1(norm_ffn_w[0]), w_router_t=w_router[0].T, b_router=b_router[0].reshape(-1, 1))
    h2, xn, topi, gate_w, rank, counts = _tail(x2d, u, gates, yg, u_meta, tail_params, batch, seq, tt_tail,
                                               n_experts)

    counts = counts[:, 0].astype(I32)
    pcounts = (counts + tm_e - 1) // tm_e * tm_e
    pends = jnp.cumsum(pcounts).astype(I32)
    pstarts = pends - pcounts
    experts = jnp.arange(n_experts, dtype=I32)
    start_of = jnp.sum(jnp.where(topi[:TOP_K, :, None] == experts, pstarts, 0), axis=-1)
    dest = start_of + rank[:TOP_K]
    n_tiles = rows * TOP_K // tm_e + n_experts
    tile_starts = jnp.arange(n_tiles, dtype=I32) * tm_e
    tile_e = jnp.minimum(jnp.sum(pends[None, :] <= tile_starts[:, None], axis=-1), n_experts - 1).astype(I32)
    n_used = (pends[-1:] // tm_e).astype(I32)

    xs = _dispatch(pends, pcounts, _tile_major(dest, td), xn, n_tiles * tm_e, td, tm_e, n_experts)
    ys = _experts(tile_e, n_used, pcounts // tm_e, xs, w_gate_up[0], b_gate_up[0][:, None, :],
                  w_down[0], b_down[0][:, None, :], tm_e)
    out = _combine(_tile_major(dest, tc_half), ys, h2, gate_w, row1(norm_final_w), tc_half)
    return out.reshape(batch, seq, d)
```

```python
import functools

import jax
import jax.numpy as jnp
import numpy as np
from jax import lax
from jax.experimental import pallas as pl
from jax.experimental.pallas import tpu as pltpu

N_META = 16
CONV_WIDTH = 31
RET_HEADS = 4
RET_DK = 256
RET_DV = 512
RET_CHUNK = 128
ROPE_BASE = 10000.0
TOP_K = 4
SWIGLU_LIMIT = 7.0
SWIGLU_ALPHA = 1.702
EPS = 1e-5

LANES = 128
SUBLANES = 8
CONV_HALO = 32
DMA_PRIORITIES = 2
VMEM_LIMIT = 56 * 1024 * 1024

F32 = jnp.float32
BF16 = jnp.bfloat16
I32 = jnp.int32

WB_CONV_A, WB_CONV_B, WB_Q, WB_V, WB_G, WB_GATE = 0, 1, 2, 4, 6, 8


def _params(n_grid_axes):
    return pltpu.CompilerParams(dimension_semantics=("arbitrary",) * n_grid_axes,
                                vmem_limit_bytes=VMEM_LIMIT)


def _silu(x):
    return x * jax.nn.sigmoid(x)


def _inproj_body(x_ref, nw_ref, w_ref, cos_ref, sin_ref, u_ref, qk_ref, v_ref, sg_ref, gate_ref):
    x = x_ref[...]
    ms = jnp.mean(x * x, axis=-1, keepdims=True)
    hn = (x * lax.rsqrt(ms + EPS) * nw_ref[...]).astype(BF16)

    def proj(block):
        return jnp.dot(hn, w_ref[:, block * 1024:(block + 1) * 1024], preferred_element_type=F32)

    u_ref[...] = (proj(WB_CONV_A) * jax.nn.sigmoid(proj(WB_CONV_B))).astype(BF16)

    half = RET_DK // 2
    for j, scale in enumerate((RET_DK ** -0.5, 1.0)):
        acc = proj(WB_Q + j)
        cos = cos_ref[...] * scale
        sin = sin_ref[...] * scale
        for h in range(RET_HEADS):
            lo = j * 1024 + h * RET_DK
            x1 = acc[:, h * RET_DK:h * RET_DK + half]
            x2 = acc[:, h * RET_DK + half:(h + 1) * RET_DK]
            qk_ref[:, lo:lo + half] = (x1 * cos - x2 * sin).astype(BF16)
            qk_ref[:, lo + half:lo + RET_DK] = (x2 * cos + x1 * sin).astype(BF16)

    for j in range(2):
        cols = slice(j * 1024, (j + 1) * 1024)
        v_ref[:, cols] = proj(WB_V + j).astype(BF16)
        sg_ref[:, cols] = _silu(proj(WB_G + j)).astype(BF16)
        gate_ref[:, cols] = jax.nn.sigmoid(proj(WB_GATE + j)).astype(BF16)


def _inproj(x2d, norm_w, w_in_bf, cos, sin, tm):
    rows, d = x2d.shape
    period_tiles = cos.shape[0] // tm
    wide = pl.BlockSpec((tm, 2048), lambda i: (i, 0))
    return pl.pallas_call(
        _inproj_body,
        out_shape=(jax.ShapeDtypeStruct((rows, 1024), BF16),) + (jax.ShapeDtypeStruct((rows, 2048), BF16),) * 4,
        grid=(rows // tm,),
        in_specs=[
            pl.BlockSpec((tm, d), lambda i: (i, 0)),
            pl.BlockSpec((1, d), lambda i: (0, 0)),
            pl.BlockSpec(w_in_bf.shape, lambda i: (0, 0), pipeline_mode=pl.Buffered(1)),
            pl.BlockSpec((tm, LANES), lambda i: (i % period_tiles, 0)),
            pl.BlockSpec((tm, LANES), lambda i: (i % period_tiles, 0)),
        ],
        out_specs=(pl.BlockSpec((tm, 1024), lambda i: (i, 0)), wide, wide, wide, wide),
        compiler_params=_params(1),
        name="inproj",
    )(x2d, norm_w, w_in_bf, cos, sin)


def _dot_t0(a, b):
    return lax.dot_general(a, b, (((0,), (0,)), ((), ())), preferred_element_type=F32)


def _dot_t1(a, b):
    return lax.dot_general(a, b, (((1,), (1,)), ((), ())), preferred_element_type=F32)


def _retention_body(cd_ref, q_ref, k_ref, v_ref, sg_ref, km_ref, vm_ref, dmat_ref, zeta_ref, xi_ref, gnw_ref,
                    o_ref, state_ref, *, n_chunks):
    t = pl.program_id(1)
    C, DK, DV = RET_CHUNK, RET_DK, RET_DV

    def state_update(prev, h, k_h, v_h):
        kz = (k_h.astype(F32) * zeta_ref[h]).astype(BF16)
        return prev * cd_ref[h] + _dot_t0(kz, v_h)

    @pl.when(t == 0)
    def _seed_from_meta_chunk():
        for h in range(RET_HEADS):
            state_ref[h] = state_update(jnp.zeros((DK, DV), F32), h,
                                        km_ref[:, h * DK:(h + 1) * DK], vm_ref[:, h * DV:(h + 1) * DV])

    def chunk(c, carry):
        rows = pl.ds(pl.multiple_of(c * C, C), C)
        for h in range(RET_HEADS):
            q_h = q_ref[rows, h * DK:(h + 1) * DK]
            k_h = k_ref[rows, h * DK:(h + 1) * DK]
            v_h = v_ref[rows, h * DV:(h + 1) * DV]
            scores = _dot_t1(q_h, k_h) * dmat_ref[h]
            inner = jnp.dot(scores.astype(BF16), v_h, preferred_element_type=F32)
            state = state_ref[h]
            cross = jnp.dot(q_h, state.astype(BF16), preferred_element_type=F32) * xi_ref[h]
            state_ref[h] = state_update(state, h, k_h, v_h)
            y = inner + cross
            mu = jnp.mean(y, axis=-1, keepdims=True)
            yc = y - mu
            yn = yc * lax.rsqrt(jnp.mean(yc * yc, axis=-1, keepdims=True) + EPS)
            gated = yn * gnw_ref[:, h * DV:(h + 1) * DV] * sg_ref[rows, h * DV:(h + 1) * DV].astype(F32)
            o_ref[rows, h * DV:(h + 1) * DV] = gated.astype(BF16)
        return carry

    lax.fori_loop(0, n_chunks, chunk, 0)


def _retention(qk, v, sg, qk_meta, v_meta, tables, gn_w, batch, seq, tt):
    chunk_decay, dmat, zeta, xi = tables
    tiles = seq // tt
    row = lambda b, t: b * tiles + t
    hv = RET_HEADS * RET_DV
    full = lambda shape: pl.BlockSpec(shape, lambda b, t: (0,) * len(shape))
    return pl.pallas_call(
        functools.partial(_retention_body, n_chunks=tt // RET_CHUNK),
        out_shape=jax.ShapeDtypeStruct((batch * seq, hv), BF16),
        grid=(batch, tiles),
        in_specs=[
            pl.BlockSpec(memory_space=pltpu.SMEM),
            pl.BlockSpec((tt, 1024), lambda b, t: (row(b, t), 0)),
            pl.BlockSpec((tt, 1024), lambda b, t: (row(b, t), 1)),
            pl.BlockSpec((tt, hv), lambda b, t: (row(b, t), 0)),
            pl.BlockSpec((tt, hv), lambda b, t: (row(b, t), 0)),
            pl.BlockSpec((RET_CHUNK, 1024), lambda b, t: (0, 1)),
            full((RET_CHUNK, hv)),
            full((RET_HEADS, RET_CHUNK, RET_CHUNK)),
            full((RET_HEADS, RET_CHUNK, 1)),
            full((RET_HEADS, RET_CHUNK, 1)),
            full((1, hv)),
        ],
        out_specs=pl.BlockSpec((tt, hv), lambda b, t: (row(b, t), 0)),
        scratch_shapes=[pltpu.VMEM((RET_HEADS, RET_DK, RET_DV), F32)],
        compiler_params=_params(2),
        name="retention",
    )(chunk_decay, qk, qk, v, sg, qk_meta, v_meta, dmat, zeta, xi, gn_w)


def _tail_body(x_ref, u_ref, gates_ref, yg_ref, um_ref, dww_ref, dwb_ref, lnw_ref, lnb_ref,
               wco_ref, bco_ref, wro_ref, wmo_ref, nfw_ref, wrt_ref, br_ref, tri_ref,
               h2_ref, xn_ref, topi_ref, gate_ref, rank_ref, cnt_ref,
               uext_ref, phase_ref, conv_ref, count_ref, *, tt, n_experts):
    b = pl.program_id(0)
    t = pl.program_id(1)
    C = RET_CHUNK
    d = x_ref.shape[1]

    @pl.when((b == 0) & (t == 0))
    def _zero_counts():
        count_ref[...] = jnp.zeros_like(count_ref)

    @pl.when(t == 0)
    def _seed_conv_history():
        uext_ref[0:CONV_HALO, :] = um_ref[C - CONV_HALO:C, :].astype(F32)

    uext_ref[CONV_HALO:CONV_HALO + tt, :] = u_ref[...].astype(F32)

    first = CONV_HALO - (CONV_WIDTH - 1)
    for s in range(d // LANES):
        cols = slice(s * LANES, (s + 1) * LANES)
        for phase in range(1, SUBLANES):
            phase_ref[phase - 1] = uext_ref[phase:phase + phase_ref.shape[1], cols]
        for c in range(tt // C):
            acc = jnp.zeros((C, LANES), F32)
            for k in range(CONV_WIDTH):
                a, phase = divmod(first + k, SUBLANES)
                r0 = c * C + a * SUBLANES
                window = uext_ref[r0:r0 + C, cols] if phase == 0 else phase_ref[phase - 1, r0:r0 + C, :]
                acc = acc + dww_ref[k:k + 1, cols] * window
            conv_ref[c * C:(c + 1) * C, cols] = acc + dwb_ref[:, cols]
    uext_ref[0:CONV_HALO, :] = uext_ref[tt:tt + CONV_HALO, :]

    cv = conv_ref[...]
    mu = jnp.mean(cv, axis=-1, keepdims=True)
    cc = cv - mu
    ln = cc * lax.rsqrt(jnp.mean(cc * cc, axis=-1, keepdims=True) + EPS) * lnw_ref[...] + lnb_ref[...]
    act = (ln * jax.nn.sigmoid(ln)).astype(BF16)
    y_conv = jnp.dot(act, wco_ref[...], preferred_element_type=F32) + bco_ref[...]
    y_ret = jnp.dot(yg_ref[...], wro_ref[...], preferred_element_type=F32)
    merged = (gates_ref[:, :d].astype(F32) * y_conv + gates_ref[:, d:].astype(F32) * y_ret).astype(BF16)
    h2 = x_ref[...] + jnp.dot(merged, wmo_ref[...], preferred_element_type=F32)
    h2_ref[...] = h2

    xn = h2 * lax.rsqrt(jnp.mean(h2 * h2, axis=-1, keepdims=True) + EPS) * nfw_ref[...]
    xn_ref[...] = xn

    logits = lax.dot_general(wrt_ref[...], xn, (((1,), (1,)), ((), ())),
                             precision=lax.Precision.HIGHEST, preferred_element_type=F32) + br_ref[...]
    eidx = lax.broadcasted_iota(I32, (n_experts, tt), 0)
    selected = jnp.zeros((n_experts, tt), F32)
    vals, idxs = [], []
    for _ in range(TOP_K):
        m = jnp.max(logits, axis=0, keepdims=True)
        idx = jnp.min(jnp.where(logits == m, eidx, n_experts), axis=0, keepdims=True)
        hit = eidx == idx
        vals.append(m)
        idxs.append(idx)
        selected = jnp.where(hit, 1.0, selected)
        logits = jnp.where(hit, -jnp.inf, logits)
    exps = [jnp.exp(v - vals[0]) for v in vals]
    denom = exps[0] + exps[1] + exps[2] + exps[3]

    before = jnp.dot(selected.astype(BF16), tri_ref[...], preferred_element_type=F32) + count_ref[:, 0:1]
    ranks = [jnp.sum(jnp.where(eidx == idx, before, 0.0), axis=0, keepdims=True).astype(I32) for idx in idxs]
    count_ref[...] = count_ref[...] + jnp.sum(selected, axis=1, keepdims=True)

    zi = jnp.zeros((8 - TOP_K, tt), I32)
    topi_ref[...] = jnp.concatenate(idxs + [zi], axis=0)
    rank_ref[...] = jnp.concatenate(ranks + [zi], axis=0)
    gate_ref[...] = jnp.concatenate([e / denom for e in exps] + [zi.astype(F32)], axis=0)
    cnt_ref[...] = count_ref[...]


def _tail(x2d, u, gates, yg, u_meta, p, batch, seq, tt, n_experts):
    d = x2d.shape[1]
    tiles = seq // tt
    rows = batch * seq
    row = lambda b, t: b * tiles + t
    full = lambda shape: pl.BlockSpec(shape, lambda b, t: (0,) * len(shape), pipeline_mode=pl.Buffered(1))
    tri = (jnp.arange(tt)[:, None] < jnp.arange(tt)[None, :]).astype(BF16)
    lane_rows = pl.BlockSpec((8, tt), lambda b, t: (0, row(b, t)))
    return pl.pallas_call(
        functools.partial(_tail_body, tt=tt, n_experts=n_experts),
        out_shape=(
            jax.ShapeDtypeStruct((rows, d), F32),
            jax.ShapeDtypeStruct((rows, d), F32),
            jax.ShapeDtypeStruct((8, rows), I32),
            jax.ShapeDtypeStruct((8, rows), F32),
            jax.ShapeDtypeStruct((8, rows), I32),
            jax.ShapeDtypeStruct((n_experts, LANES), F32),
        ),
        grid=(batch, tiles),
        in_specs=[
            pl.BlockSpec((tt, d), lambda b, t: (row(b, t), 0)),
            pl.BlockSpec((tt, 1024), lambda b, t: (row(b, t), 0)),
            pl.BlockSpec((tt, 2048), lambda b, t: (row(b, t), 0)),
            pl.BlockSpec((tt, yg.shape[1]), lambda b, t: (row(b, t), 0)),
            full((RET_CHUNK, 1024)),
            full((CONV_WIDTH, d)), full((1, d)), full((1, d)), full((1, d)),
            full((d, d)), full((1, d)),
            full((yg.shape[1], d)),
            full((d, d)),
            full((1, d)),
            full((n_experts, d)), full((n_experts, 1)),
            full((tt, tt)),
        ],
        out_specs=(
            pl.BlockSpec((tt, d), lambda b, t: (row(b, t), 0)),
            pl.BlockSpec((tt, d), lambda b, t: (row(b, t), 0)),
            lane_rows, lane_rows, lane_rows,
            pl.BlockSpec((n_experts, LANES), lambda b, t: (0, 0)),
        ),
        scratch_shapes=[
            pltpu.VMEM((CONV_HALO + tt, d), F32),
            pltpu.VMEM((SUBLANES - 1, CONV_HALO + tt - SUBLANES, LANES), F32),
            pltpu.VMEM((tt, d), F32),
            pltpu.VMEM((n_experts, LANES), F32),
        ],
        compiler_params=_params(2),
        name="mixer_tail",
    )(x2d, u, gates, yg, u_meta, p["dw_w"], p["dw_b"], p["ln_w"], p["ln_b"], p["w_conv_out"], p["b_conv_out"],
      p["w_ret_out"], p["w_mix_out"], p["norm_ffn_w"], p["w_router_t"], p["b_router"], tri)


def _dispatch_body(pend_ref, pcnt_ref, dest_ref, xn_ref, xs_hbm, zero_ref, sem, zsem, *, td, tm, n_experts):
    def zero_copy(e):
        start = pl.multiple_of(pend_ref[e] - tm, tm)
        return pltpu.make_async_copy(zero_ref, xs_hbm.at[pl.ds(start, tm)], zsem)

    @pl.when(pl.program_id(0) == 0)
    def _zero_last_tile_of_every_expert():
        zero_ref[...] = jnp.zeros_like(zero_ref)
        for e in range(n_experts):
            pl.when(pcnt_ref[e] > 0)(lambda e=e: zero_copy(e).start())
        for e in range(n_experts):
            pl.when(pcnt_ref[e] > 0)(lambda e=e: zero_copy(e).wait())

    def row_copy(r, dst_row):
        return pltpu.make_async_copy(xn_ref.at[pl.ds(r, 1)], xs_hbm.at[pl.ds(dst_row, 1)], sem)

    for r in range(td):
        for k in range(TOP_K):
            row_copy(r, dest_ref[k * td + r]).start(priority=k % DMA_PRIORITIES)

    def drain(r, carry):
        for k in range(TOP_K):
            row_copy(0, 0).wait()
        return carry

    lax.fori_loop(0, td, drain, 0, unroll=8)


def _dispatch(pends, pcounts, dest_tiles, xn, n_rows_out, td, tm, n_experts):
    rows, w = xn.shape
    return pl.pallas_call(
        functools.partial(_dispatch_body, td=td, tm=tm, n_experts=n_experts),
        out_shape=jax.ShapeDtypeStruct((n_rows_out, w), F32),
        grid_spec=pltpu.PrefetchScalarGridSpec(
            num_scalar_prefetch=2,
            grid=(rows // td,),
            in_specs=[
                pl.BlockSpec((TOP_K * td,), lambda i, *_: (i,), memory_space=pltpu.SMEM),
                pl.BlockSpec((td, w), lambda i, *_: (i, 0)),
            ],
            out_specs=pl.BlockSpec(memory_space=pl.ANY),
            scratch_shapes=[pltpu.VMEM((tm, w), F32), pltpu.SemaphoreType.DMA, pltpu.SemaphoreType.DMA],
        ),
        compiler_params=_params(1),
        name="moe_dispatch",
    )(pends, pcounts, dest_tiles, xn)


def _experts_body(tile_e_ref, n_used_ref, ptiles_ref, xs_ref, wgu_hbm, bgu_ref, wd_hbm, bd_ref, ys_ref,
                  wgu_stage_ref, wd_stage_ref, wgu_bf_ref, wd_bf_ref, sems):
    p = pl.program_id(0)
    n_used = n_used_ref[0]

    def weight_copies(e):
        return (pltpu.make_async_copy(wgu_hbm.at[e], wgu_stage_ref, sems.at[0]),
                pltpu.make_async_copy(wd_hbm.at[e], wd_stage_ref, sems.at[1]))

    @pl.when(p < n_used)
    def _():
        e = tile_e_ref[p]

        @pl.when(p == 0)
        def _fetch_first_expert():
            for c in weight_copies(e):
                c.start()

        @pl.when((p == 0) | (e != tile_e_ref[jnp.maximum(p - 1, 0)]))
        def _switch_expert():
            for c in weight_copies(e):
                c.wait()
            wgu_bf_ref[...] = wgu_stage_ref[...].astype(BF16)
            wd_bf_ref[...] = wd_stage_ref[...].astype(BF16)
            nxt = p + ptiles_ref[e]

            @pl.when(nxt < n_used)
            def _fetch_next_expert():
                for c in weight_copies(tile_e_ref[nxt]):
                    c.start()

        x = xs_ref[...].astype(BF16)
        hgu = jnp.dot(x, wgu_bf_ref[...], preferred_element_type=F32) + bgu_ref[0]
        de = hgu.shape[1] // 2
        h_gate = jnp.minimum(hgu[:, :de], SWIGLU_LIMIT)
        h_up = jnp.clip(hgu[:, de:], -SWIGLU_LIMIT, SWIGLU_LIMIT)
        act = h_gate * jax.nn.sigmoid(SWIGLU_ALPHA * h_gate) * (h_up + 1.0)
        y = jnp.dot(act.astype(BF16), wd_bf_ref[...], preferred_element_type=F32) + bd_ref[0]
        ys_ref[...] = y


def _experts(tile_e, n_used, ptiles, xs, wgu, bgu, wd, bd, tm):
    n_rows, w = xs.shape
    e, d, de2 = wgu.shape
    tile = lambda p, te, nu, pt: jnp.minimum(p, nu[0] - 1)
    expert = lambda p, te, nu, pt: te[tile(p, te, nu, pt)]
    return pl.pallas_call(
        _experts_body,
        out_shape=jax.ShapeDtypeStruct((n_rows, w), F32),
        grid_spec=pltpu.PrefetchScalarGridSpec(
            num_scalar_prefetch=3,
            grid=(n_rows // tm,),
            in_specs=[
                pl.BlockSpec((tm, w), lambda p, te, nu, pt: (tile(p, te, nu, pt), 0)),
                pl.BlockSpec(memory_space=pl.ANY),
                pl.BlockSpec((1, 1, de2), lambda p, te, nu, pt: (expert(p, te, nu, pt), 0, 0)),
                pl.BlockSpec(memory_space=pl.ANY),
                pl.BlockSpec((1, 1, d), lambda p, te, nu, pt: (expert(p, te, nu, pt), 0, 0)),
            ],
            out_specs=pl.BlockSpec((tm, w), lambda p, te, nu, pt: (tile(p, te, nu, pt), 0)),
            scratch_shapes=[pltpu.VMEM((d, de2), F32), pltpu.VMEM((de2 // 2, d), F32),
                            pltpu.VMEM((d, de2), BF16), pltpu.VMEM((de2 // 2, d), BF16),
                            pltpu.SemaphoreType.DMA((2,))],
        ),
        compiler_params=_params(1),
        name="moe_experts",
    )(tile_e, n_used, ptiles, xs, wgu, bgu, wd, bd)


def _combine_body(dest_ref, dest_next_ref, ys_hbm, h2_ref, gate_ref, nw_ref, o_ref, buf_ref, sems, *, th):
    i = pl.program_id(0)
    half_words = TOP_K * th

    def row_copy(src_row, slot, k, r):
        return pltpu.make_async_copy(ys_hbm.at[pl.ds(src_row, 1)], buf_ref.at[slot, k, pl.ds(r, 1)],
                                     sems.at[slot])

    def issue(idx_ref, base, slot):
        for r in range(th):
            for k in range(TOP_K):
                row_copy(idx_ref[base + k * th + r], slot, k, r).start(priority=k % DMA_PRIORITIES)

    def drain(slot):
        def body(r, carry):
            for k in range(TOP_K):
                row_copy(0, slot, k, 0).wait()
            return carry
        lax.fori_loop(0, th, body, 0, unroll=8)

    g = gate_ref[...]
    g_cols = jnp.concatenate([g, jnp.zeros((LANES - g.shape[0], 2 * th), F32)], axis=0).T

    def finish(half):
        rows = slice(half * th, (half + 1) * th)
        h = h2_ref[rows, :]
        for k in range(TOP_K):
            h = h + g_cols[rows, k:k + 1] * buf_ref[half, k]
        o_ref[rows, :] = h * lax.rsqrt(jnp.mean(h * h, axis=-1, keepdims=True) + EPS) * nw_ref[...]

    @pl.when(i == 0)
    def _first_half_tile():
        def body(r, carry):
            for k in range(TOP_K):
                row_copy(dest_ref[k * th + r], 0, k, r).start(priority=k % DMA_PRIORITIES)
            return carry
        lax.fori_loop(0, th, body, 0, unroll=8)

    issue(dest_ref, half_words, 1)
    drain(0)
    finish(0)

    @pl.when(i + 1 < pl.num_programs(0))
    def _next_step_first_half_tile():
        issue(dest_next_ref, 0, 0)

    drain(1)
    finish(1)


def _combine(dest_tiles, ys, h2, gates, norm_w, th):
    rows, d = h2.shape
    steps = rows // (2 * th)
    return pl.pallas_call(
        functools.partial(_combine_body, th=th),
        out_shape=jax.ShapeDtypeStruct((rows, d), F32),
        grid=(steps,),
        in_specs=[
            pl.BlockSpec((2 * TOP_K * th,), lambda i: (i,), memory_space=pltpu.SMEM),
            pl.BlockSpec((TOP_K * th,), lambda i: (jnp.minimum(2 * i + 2, 2 * steps - 1),),
                         memory_space=pltpu.SMEM),
            pl.BlockSpec(memory_space=pl.ANY),
            pl.BlockSpec((2 * th, d), lambda i: (i, 0)),
            pl.BlockSpec((8, 2 * th), lambda i: (0, i)),
            pl.BlockSpec((1, d), lambda i: (0, 0)),
        ],
        out_specs=pl.BlockSpec((2 * th, d), lambda i: (i, 0)),
        scratch_shapes=[pltpu.VMEM((2, TOP_K, th, d), F32), pltpu.SemaphoreType.DMA((2,))],
        compiler_params=_params(1),
        name="moe_combine",
    )(dest_tiles, dest_tiles, ys, h2, gates, norm_w)


def _rope_tables(positions):
    inv_freq = ROPE_BASE ** (-np.arange(0, RET_DK, 2, dtype=np.float64) / RET_DK)
    ang = np.asarray(positions, np.float64)[:, None] * inv_freq[None, :]
    return jnp.asarray(np.cos(ang), F32), jnp.asarray(np.sin(ang), F32)


def _retention_tables():
    c = RET_CHUNK
    log_gamma = jnp.log1p(-jnp.exp2(-5.0 - jnp.arange(RET_HEADS, dtype=F32)))
    idx = jnp.arange(c, dtype=F32)
    diff = idx[:, None] - idx[None, :]
    dmat = jnp.where(diff[None] >= 0, jnp.exp(log_gamma[:, None, None] * jnp.maximum(diff, 0.0)[None]), 0.0)
    zeta = jnp.exp(log_gamma[:, None] * (c - 1 - idx)[None, :])[:, :, None]
    xi = jnp.exp(log_gamma[:, None] * (idx + 1.0)[None, :])[:, :, None]
    chunk_decay = jnp.exp(log_gamma * c)
    return chunk_decay, dmat, zeta, xi


def _largest_tile(n, cap):
    t = cap
    while n % t:
        t //= 2
    return t


def _tile_major(a, tile):
    k, t = a.shape
    return a.reshape(k, t // tile, tile).transpose(1, 0, 2).reshape(-1)


def kernel(x, meta_tokens, norm_mix_w, w_in, conv_dw_w, conv_dw_b, conv_ln_w, conv_ln_b, w_conv_out,
           b_conv_out, ret_gn_w, w_ret_out, w_mix_out, norm_ffn_w, w_router, b_router, w_gate_up,
           b_gate_up, w_down, b_down, norm_final_w):
    batch, seq, d = x.shape
    depth = w_in.shape[0]
    n_experts = w_router.shape[-1]
    assert depth == 1 and d == 1024 and seq % RET_CHUNK == 0
    assert w_in.shape[-1] == 10 * 1024 and w_gate_up.shape[-1] == 2 * d
    rows = batch * seq

    tm_in = _largest_tile(seq, 512)
    tt_ret = _largest_tile(seq, 512)
    tt_tail = _largest_tile(seq, 512)
    td = _largest_tile(rows, 512)
    tc_half = _largest_tile(rows, 512) // 2
    tm_e = 512

    x2d = x.reshape(rows, d)
    w_in_bf = w_in[0].astype(BF16)
    row1 = lambda v: v.reshape(1, -1)
    norm_w = row1(norm_mix_w[0])

    cos, sin = _rope_tables(N_META + np.arange(seq))
    u, qk, v, sg, gates = _inproj(x2d, norm_w, w_in_bf, cos, sin, tm_in)

    pad = RET_CHUNK - N_META
    meta_chunk = jnp.concatenate([jnp.zeros((pad, d), x.dtype), meta_tokens.astype(x.dtype)], axis=0)
    cos_m, sin_m = _rope_tables(np.maximum(np.arange(RET_CHUNK) - pad, 0))
    u_meta, qk_meta, v_meta, _, _ = _inproj(meta_chunk, norm_w, w_in_bf, cos_m, sin_m, RET_CHUNK)

    yg = _retention(qk, v, sg, qk_meta, v_meta, _retention_tables(), row1(ret_gn_w[0]), batch, seq, tt_ret)

    tail_params = dict(
        dw_w=conv_dw_w[0], dw_b=row1(conv_dw_b[0]), ln_w=row1(conv_ln_w[0]), ln_b=row1(conv_ln_b[0]),
        w_conv_out=w_conv_out[0].astype(BF16), b_conv_out=row1(b_conv_out[0]),
        w_ret_out=w_ret_out[0].astype(BF16), w_mix_out=w_mix_out[0].astype(BF16),
        norm_ffn_w=row1(norm_ffn_w[0]), w_router_t=w_router[0].T, b_router=b_router[0].reshape(-1, 1))
    h2, xn, topi, gate_w, rank, counts = _tail(x2d, u, gates, yg, u_meta, tail_params, batch, seq, tt_tail,
                                               n_experts)

    counts = counts[:, 0].astype(I32)
    pcounts = (counts + tm_e - 1) // tm_e * tm_e
    pends = jnp.cumsum(pcounts).astype(I32)
    pstarts = pends - pcounts
    experts = jnp.arange(n_experts, dtype=I32)
    start_of = jnp.sum(jnp.where(topi[:TOP_K, :, None] == experts, pstarts, 0), axis=-1)
    dest = start_of + rank[:TOP_K]
    n_tiles = rows * TOP_K // tm_e + n_experts
    tile_starts = jnp.arange(n_tiles, dtype=I32) * tm_e
    tile_e = jnp.minimum(jnp.sum(pends[None, :] <= tile_starts[:, None], axis=-1), n_experts - 1).astype(I32)
    n_used = (pends[-1:] // tm_e).astype(I32)

    xs = _dispatch(pends, pcounts, _tile_major(dest, td), xn, n_tiles * tm_e, td, tm_e, n_experts)
    ys = _experts(tile_e, n_used, pcounts // tm_e, xs, w_gate_up[0], b_gate_up[0][:, None, :],
                  w_down[0], b_down[0][:, None, :], tm_e)
    out = _combine(_tile_major(dest, tc_half), ys, h2, gate_w, row1(norm_final_w), tc_half)
    return out.reshape(batch, seq, d)
```

```python
import functools

import jax
import jax.numpy as jnp
import numpy as np
from jax import lax
from jax.experimental import pallas as pl
from jax.experimental.pallas import tpu as pltpu

N_META = 16
CONV_WIDTH = 31
RET_HEADS = 4
RET_DK = 256
RET_DV = 512
RET_CHUNK = 128
ROPE_BASE = 10000.0
TOP_K = 4
SWIGLU_LIMIT = 7.0
SWIGLU_ALPHA = 1.702
EPS = 1e-5

LANES = 128
SUBLANES = 8
CONV_HALO = 32
DMA_PRIORITIES = 2
VMEM_LIMIT = 56 * 1024 * 1024

F32 = jnp.float32
BF16 = jnp.bfloat16
I32 = jnp.int32

WB_CONV_A, WB_CONV_B, WB_Q, WB_V, WB_G, WB_GATE = 0, 1, 2, 4, 6, 8


def _params(n_grid_axes):
    return pltpu.CompilerParams(dimension_semantics=("arbitrary",) * n_grid_axes,
                                vmem_limit_bytes=VMEM_LIMIT)


def _silu(x):
    return x * jax.nn.sigmoid(x)


def _inproj_body(x_ref, nw_ref, w_ref, cos_ref, sin_ref, u_ref, qk_ref, v_ref, sg_ref, gate_ref):
    x = x_ref[...]
    ms = jnp.mean(x * x, axis=-1, keepdims=True)
    hn = (x * lax.rsqrt(ms + EPS) * nw_ref[...]).astype(BF16)

    def proj(block):
        return jnp.dot(hn, w_ref[:, block * 1024:(block + 1) * 1024], preferred_element_type=F32)

    u_ref[...] = (proj(WB_CONV_A) * jax.nn.sigmoid(proj(WB_CONV_B))).astype(BF16)

    half = RET_DK // 2
    for j, scale in enumerate((RET_DK ** -0.5, 1.0)):
        acc = proj(WB_Q + j)
        cos = cos_ref[...] * scale
        sin = sin_ref[...] * scale
        for h in range(RET_HEADS):
            lo = j * 1024 + h * RET_DK
            x1 = acc[:, h * RET_DK:h * RET_DK + half]
            x2 = acc[:, h * RET_DK + half:(h + 1) * RET_DK]
            qk_ref[:, lo:lo + half] = (x1 * cos - x2 * sin).astype(BF16)
            qk_ref[:, lo + half:lo + RET_DK] = (x2 * cos + x1 * sin).astype(BF16)

    for j in range(2):
        cols = slice(j * 1024, (j + 1) * 1024)
        v_ref[:, cols] = proj(WB_V + j).astype(BF16)
        sg_ref[:, cols] = _silu(proj(WB_G + j)).astype(BF16)
        gate_ref[:, cols] = jax.nn.sigmoid(proj(WB_GATE + j)).astype(BF16)


def _inproj(x2d, norm_w, w_in_bf, cos, sin, tm):
    rows, d = x2d.shape
    period_tiles = cos.shape[0] // tm
    wide = pl.BlockSpec((tm, 2048), lambda i: (i, 0))
    return pl.pallas_call(
        _inproj_body,
        out_shape=(jax.ShapeDtypeStruct((rows, 1024), BF16),) + (jax.ShapeDtypeStruct((rows, 2048), BF16),) * 4,
        grid=(rows // tm,),
        in_specs=[
            pl.BlockSpec((tm, d), lambda i: (i, 0)),
            pl.BlockSpec((1, d), lambda i: (0, 0)),
            pl.BlockSpec(w_in_bf.shape, lambda i: (0, 0), pipeline_mode=pl.Buffered(1)),
            pl.BlockSpec((tm, LANES), lambda i: (i % period_tiles, 0)),
            pl.BlockSpec((tm, LANES), lambda i: (i % period_tiles, 0)),
        ],
        out_specs=(pl.BlockSpec((tm, 1024), lambda i: (i, 0)), wide, wide, wide, wide),
        compiler_params=_params(1),
        name="inproj",
    )(x2d, norm_w, w_in_bf, cos, sin)


def _dot_t0(a, b):
    return lax.dot_general(a, b, (((0,), (0,)), ((), ())), preferred_element_type=F32)


def _dot_t1(a, b):
    return lax.dot_general(a, b, (((1,), (1,)), ((), ())), preferred_element_type=F32)


def _retention_body(cd_ref, q_ref, k_ref, v_ref, sg_ref, km_ref, vm_ref, dmat_ref, zeta_ref, xi_ref, gnw_ref,
                    o_ref, state_ref, *, n_chunks):
    t = pl.program_id(1)
    C, DK, DV = RET_CHUNK, RET_DK, RET_DV

    def state_update(prev, h, k_h, v_h):
        kz = (k_h.astype(F32) * zeta_ref[h]).astype(BF16)
        return prev * cd_ref[h] + _dot_t0(kz, v_h)

    @pl.when(t == 0)
    def _seed_from_meta_chunk():
        for h in range(RET_HEADS):
            state_ref[h] = state_update(jnp.zeros((DK, DV), F32), h,
                                        km_ref[:, h * DK:(h + 1) * DK], vm_ref[:, h * DV:(h + 1) * DV])

    def chunk(c, carry):
        rows = pl.ds(pl.multiple_of(c * C, C), C)
        for h in range(RET_HEADS):
            q_h = q_ref[rows, h * DK:(h + 1) * DK]
            k_h = k_ref[rows, h * DK:(h + 1) * DK]
            v_h = v_ref[rows, h * DV:(h + 1) * DV]
            scores = _dot_t1(q_h, k_h) * dmat_ref[h]
            inner = jnp.dot(scores.astype(BF16), v_h, preferred_element_type=F32)
            state = state_ref[h]
            cross = jnp.dot(q_h, state.astype(BF16), preferred_element_type=F32) * xi_ref[h]
            state_ref[h] = state_update(state, h, k_h, v_h)
            y = inner + cross
            mu = jnp.mean(y, axis=-1, keepdims=True)
            yc = y - mu
            yn = yc * lax.rsqrt(jnp.mean(yc * yc, axis=-1, keepdims=True) + EPS)
            gated = yn * gnw_ref[:, h * DV:(h + 1) * DV] * sg_ref[rows, h * DV:(h + 1) * DV].astype(F32)
            o_ref[rows, h * DV:(h + 1) * DV] = gated.astype(BF16)
        return carry

    lax.fori_loop(0, n_chunks, chunk, 0)


def _retention(qk, v, sg, qk_meta, v_meta, tables, gn_w, batch, seq, tt):
    chunk_decay, dmat, zeta, xi = tables
    tiles = seq // tt
    row = lambda b, t: b * tiles + t
    hv = RET_HEADS * RET_DV
    full = lambda shape: pl.BlockSpec(shape, lambda b, t: (0,) * len(shape))
    return pl.pallas_call(
        functools.partial(_retention_body, n_chunks=tt // RET_CHUNK),
        out_shape=jax.ShapeDtypeStruct((batch * seq, hv), BF16),
        grid=(batch, tiles),
        in_specs=[
            pl.BlockSpec(memory_space=pltpu.SMEM),
            pl.BlockSpec((tt, 1024), lambda b, t: (row(b, t), 0)),
            pl.BlockSpec((tt, 1024), lambda b, t: (row(b, t), 1)),
            pl.BlockSpec((tt, hv), lambda b, t: (row(b, t), 0)),
            pl.BlockSpec((tt, hv), lambda b, t: (row(b, t), 0)),
            pl.BlockSpec((RET_CHUNK, 1024), lambda b, t: (0, 1)),
            full((RET_CHUNK, hv)),
            full((RET_HEADS, RET_CHUNK, RET_CHUNK)),
            full((RET_HEADS, RET_CHUNK, 1)),
            full((RET_HEADS, RET_CHUNK, 1)),
            full((1, hv)),
        ],
        out_specs=pl.BlockSpec((tt, hv), lambda b, t: (row(b, t), 0)),
        scratch_shapes=[pltpu.VMEM((RET_HEADS, RET_DK, RET_DV), F32)],
        compiler_params=_params(2),
        name="retention",
    )(chunk_decay, qk, qk, v, sg, qk_meta, v_meta, dmat, zeta, xi, gn_w)


def _tail_body(x_ref, u_ref, gates_ref, yg_ref, um_ref, dww_ref, dwb_ref, lnw_ref, lnb_ref,
               wco_ref, bco_ref, wro_ref, wmo_ref, nfw_ref, wrt_ref, br_ref, tri_ref,
               h2_ref, xn_ref, topi_ref, gate_ref, rank_ref, cnt_ref,
               uext_ref, phase_ref, conv_ref, count_ref, *, tt, n_experts):
    b = pl.program_id(0)
    t = pl.program_id(1)
    C = RET_CHUNK
    d = x_ref.shape[1]

    @pl.when((b == 0) & (t == 0))
    def _zero_counts():
        count_ref[...] = jnp.zeros_like(count_ref)

    @pl.when(t == 0)
    def _seed_conv_history():
        uext_ref[0:CONV_HALO, :] = um_ref[C - CONV_HALO:C, :].astype(F32)

    uext_ref[CONV_HALO:CONV_HALO + tt, :] = u_ref[...].astype(F32)

    first = CONV_HALO - (CONV_WIDTH - 1)
    for s in range(d // LANES):
        cols = slice(s * LANES, (s + 1) * LANES)
        for phase in range(1, SUBLANES):
            phase_ref[phase - 1] = uext_ref[phase:phase + phase_ref.shape[1], cols]
        for c in range(tt // C):
            acc = jnp.zeros((C, LANES), F32)
            for k in range(CONV_WIDTH):
                a, phase = divmod(first + k, SUBLANES)
                r0 = c * C + a * SUBLANES
                window = uext_ref[r0:r0 + C, cols] if phase == 0 else phase_ref[phase - 1, r0:r0 + C, :]
                acc = acc + dww_ref[k:k + 1, cols] * window
            conv_ref[c * C:(c + 1) * C, cols] = acc + dwb_ref[:, cols]
    uext_ref[0:CONV_HALO, :] = uext_ref[tt:tt + CONV_HALO, :]

    cv = conv_ref[...]
    mu = jnp.mean(cv, axis=-1, keepdims=True)
    cc = cv - mu
    ln = cc * lax.rsqrt(jnp.mean(cc * cc, axis=-1, keepdims=True) + EPS) * lnw_ref[...] + lnb_ref[...]
    act = (ln * jax.nn.sigmoid(ln)).astype(BF16)
    y_conv = jnp.dot(act, wco_ref[...], preferred_element_type=F32) + bco_ref[...]
    y_ret = jnp.dot(yg_ref[...], wro_ref[...], preferred_element_type=F32)
    merged = (gates_ref[:, :d].astype(F32) * y_conv + gates_ref[:, d:].astype(F32) * y_ret).astype(BF16)
    h2 = x_ref[...] + jnp.dot(merged, wmo_ref[...], preferred_element_type=F32)
    h2_ref[...] = h2

    xn = h2 * lax.rsqrt(jnp.mean(h2 * h2, axis=-1, keepdims=True) + EPS) * nfw_ref[...]
    xn_ref[...] = xn

    logits = lax.dot_general(wrt_ref[...], xn, (((1,), (1,)), ((), ())),
                             precision=lax.Precision.HIGHEST, preferred_element_type=F32) + br_ref[...]
    eidx = lax.broadcasted_iota(I32, (n_experts, tt), 0)
    selected = jnp.zeros((n_experts, tt), F32)
    vals, idxs = [], []
    for _ in range(TOP_K):
        m = jnp.max(logits, axis=0, keepdims=True)
        idx = jnp.min(jnp.where(logits == m, eidx, n_experts), axis=0, keepdims=True)
        hit = eidx == idx
        vals.append(m)
        idxs.append(idx)
        selected = jnp.where(hit, 1.0, selected)
        logits = jnp.where(hit, -jnp.inf, logits)
    exps = [jnp.exp(v - vals[0]) for v in vals]
    denom = exps[0] + exps[1] + exps[2] + exps[3]

    before = jnp.dot(selected.astype(BF16), tri_ref[...], preferred_element_type=F32) + count_ref[:, 0:1]
    ranks = [jnp.sum(jnp.where(eidx == idx, before, 0.0), axis=0, keepdims=True).astype(I32) for idx in idxs]
    count_ref[...] = count_ref[...] + jnp.sum(selected, axis=1, keepdims=True)

    zi = jnp.zeros((8 - TOP_K, tt), I32)
    topi_ref[...] = jnp.concatenate(idxs + [zi], axis=0)
    rank_ref[...] = jnp.concatenate(ranks + [zi], axis=0)
    gate_ref[...] = jnp.concatenate([e / denom for e in exps] + [zi.astype(F32)], axis=0)
    cnt_ref[...] = count_ref[...]


def _tail(x2d, u, gates, yg, u_meta, p, batch, seq, tt, n_experts):
    d = x2d.shape[1]
    tiles = seq // tt
    rows = batch * seq
    row = lambda b, t: b * tiles + t
    full = lambda shape: pl.BlockSpec(shape, lambda b, t: (0,) * len(shape), pipeline_mode=pl.Buffered(1))
    tri = (jnp.arange(tt)[:, None] < jnp.arange(tt)[None, :]).astype(BF16)
    lane_rows = pl.BlockSpec((8, tt), lambda b, t: (0, row(b, t)))
    return pl.pallas_call(
        functools.partial(_tail_body, tt=tt, n_experts=n_experts),
        out_shape=(
            jax.ShapeDtypeStruct((rows, d), F32),
            jax.ShapeDtypeStruct((rows, d), F32),
            jax.ShapeDtypeStruct((8, rows), I32),
            jax.ShapeDtypeStruct((8, rows), F32),
            jax.ShapeDtypeStruct((8, rows), I32),
            jax.ShapeDtypeStruct((n_experts, LANES), F32),
        ),
        grid=(batch, tiles),
        in_specs=[
            pl.BlockSpec((tt, d), lambda b, t: (row(b, t), 0)),
            pl.BlockSpec((tt, 1024), lambda b, t: (row(b, t), 0)),
            pl.BlockSpec((tt, 2048), lambda b, t: (row(b, t), 0)),
            pl.BlockSpec((tt, yg.shape[1]), lambda b, t: (row(b, t), 0)),
            full((RET_CHUNK, 1024)),
            full((CONV_WIDTH, d)), full((1, d)), full((1, d)), full((1, d)),
            full((d, d)), full((1, d)),
            full((yg.shape[1], d)),
            full((d, d)),
            full((1, d)),
            full((n_experts, d)), full((n_experts, 1)),
            full((tt, tt)),
        ],
        out_specs=(
            pl.BlockSpec((tt, d), lambda b, t: (row(b, t), 0)),
            pl.BlockSpec((tt, d), lambda b, t: (row(b, t), 0)),
            lane_rows, lane_rows, lane_rows,
            pl.BlockSpec((n_experts, LANES), lambda b, t: (0, 0)),
        ),
        scratch_shapes=[
            pltpu.VMEM((CONV_HALO + tt, d), F32),
            pltpu.VMEM((SUBLANES - 1, CONV_HALO + tt - SUBLANES, LANES), F32),
            pltpu.VMEM((tt, d), F32),
            pltpu.VMEM((n_experts, LANES), F32),
        ],
        compiler_params=_params(2),
        name="mixer_tail",
    )(x2d, u, gates, yg, u_meta, p["dw_w"], p["dw_b"], p["ln_w"], p["ln_b"], p["w_conv_out"], p["b_conv_out"],
      p["w_ret_out"], p["w_mix_out"], p["norm_ffn_w"], p["w_router_t"], p["b_router"], tri)


def _dispatch_body(pend_ref, pcnt_ref, dest_ref, xn_ref, xs_hbm, zero_ref, sem, zsem, *, td, tm, n_experts):
    def zero_copy(e):
        start = pl.multiple_of(pend_ref[e] - tm, tm)
        return pltpu.make_async_copy(zero_ref, xs_hbm.at[pl.ds(start, tm)], zsem)

    @pl.when(pl.program_id(0) == 0)
    def _zero_last_tile_of_every_expert():
        zero_ref[...] = jnp.zeros_like(zero_ref)
        for e in range(n_experts):
            pl.when(pcnt_ref[e] > 0)(lambda e=e: zero_copy(e).start())
        for e in range(n_experts):
            pl.when(pcnt_ref[e] > 0)(lambda e=e: zero_copy(e).wait())

    def row_copy(r, dst_row):
        return pltpu.make_async_copy(xn_ref.at[pl.ds(r, 1)], xs_hbm.at[pl.ds(dst_row, 1)], sem)

    for r in range(td):
        for k in range(TOP_K):
            row_copy(r, dest_ref[k * td + r]).start(priority=k % DMA_PRIORITIES)

    def drain(r, carry):
        for k in range(TOP_K):
            row_copy(0, 0).wait()
        return carry

    lax.fori_loop(0, td, drain, 0, unroll=8)


def _dispatch(pends, pcounts, dest_tiles, xn, n_rows_out, td, tm, n_experts):
    rows, w = xn.shape
    return pl.pallas_call(
        functools.partial(_dispatch_body, td=td, tm=tm, n_experts=n_experts),
        out_shape=jax.ShapeDtypeStruct((n_rows_out, w), F32),
        grid_spec=pltpu.PrefetchScalarGridSpec(
            num_scalar_prefetch=2,
            grid=(rows // td,),
            in_specs=[
                pl.BlockSpec((TOP_K * td,), lambda i, *_: (i,), memory_space=pltpu.SMEM),
                pl.BlockSpec((td, w), lambda i, *_: (i, 0)),
            ],
            out_specs=pl.BlockSpec(memory_space=pl.ANY),
            scratch_shapes=[pltpu.VMEM((tm, w), F32), pltpu.SemaphoreType.DMA, pltpu.SemaphoreType.DMA],
        ),
        compiler_params=_params(1),
        name="moe_dispatch",
    )(pends, pcounts, dest_tiles, xn)


def _experts_body(tile_e_ref, n_used_ref, ptiles_ref, xs_ref, wgu_hbm, bgu_ref, wd_hbm, bd_ref, ys_ref,
                  wgu_stage_ref, wd_stage_ref, wgu_bf_ref, wd_bf_ref, sems):
    p = pl.program_id(0)
    n_used = n_used_ref[0]

    def weight_copies(e):
        return (pltpu.make_async_copy(wgu_hbm.at[e], wgu_stage_ref, sems.at[0]),
                pltpu.make_async_copy(wd_hbm.at[e], wd_stage_ref, sems.at[1]))

    @pl.when(p < n_used)
    def _():
        e = tile_e_ref[p]

        @pl.when(p == 0)
        def _fetch_first_expert():
            for c in weight_copies(e):
                c.start()

        @pl.when((p == 0) | (e != tile_e_ref[jnp.maximum(p - 1, 0)]))
        def _switch_expert():
            for c in weight_copies(e):
                c.wait()
            wgu_bf_ref[...] = wgu_stage_ref[...].astype(BF16)
            wd_bf_ref[...] = wd_stage_ref[...].astype(BF16)
            nxt = p + ptiles_ref[e]

            @pl.when(nxt < n_used)
            def _fetch_next_expert():
                for c in weight_copies(tile_e_ref[nxt]):
                    c.start()

        x = xs_ref[...].astype(BF16)
        hgu = jnp.dot(x, wgu_bf_ref[...], preferred_element_type=F32) + bgu_ref[0]
        de = hgu.shape[1] // 2
        h_gate = jnp.minimum(hgu[:, :de], SWIGLU_LIMIT)
        h_up = jnp.clip(hgu[:, de:], -SWIGLU_LIMIT, SWIGLU_LIMIT)
        act = h_gate * jax.nn.sigmoid(SWIGLU_ALPHA * h_gate) * (h_up + 1.0)
        y = jnp.dot(act.astype(BF16), wd_bf_ref[...], preferred_element_type=F32) + bd_ref[0]
        ys_ref[:, 0, :] = y


def _experts(tile_e, n_used, ptiles, xs, wgu, bgu, wd, bd, tm):
    n_rows, w = xs.shape
    e, d, de2 = wgu.shape
    tile = lambda p, te, nu, pt: jnp.minimum(p, nu[0] - 1)
    expert = lambda p, te, nu, pt: te[tile(p, te, nu, pt)]
    return pl.pallas_call(
        _experts_body,
        out_shape=jax.ShapeDtypeStruct((n_rows, 1, w), F32),
        grid_spec=pltpu.PrefetchScalarGridSpec(
            num_scalar_prefetch=3,
            grid=(n_rows // tm,),
            in_specs=[
                pl.BlockSpec((tm, w), lambda p, te, nu, pt: (tile(p, te, nu, pt), 0)),
                pl.BlockSpec(memory_space=pl.ANY),
                pl.BlockSpec((1, 1, de2), lambda p, te, nu, pt: (expert(p, te, nu, pt), 0, 0)),
                pl.BlockSpec(memory_space=pl.ANY),
                pl.BlockSpec((1, 1, d), lambda p, te, nu, pt: (expert(p, te, nu, pt), 0, 0)),
            ],
            out_specs=pl.BlockSpec((tm, 1, w), lambda p, te, nu, pt: (tile(p, te, nu, pt), 0, 0)),
            scratch_shapes=[pltpu.VMEM((d, de2), F32), pltpu.VMEM((de2 // 2, d), F32),
                            pltpu.VMEM((d, de2), BF16), pltpu.VMEM((de2 // 2, d), BF16),
                            pltpu.SemaphoreType.DMA((2,))],
        ),
        compiler_params=_params(1),
        name="moe_experts",
    )(tile_e, n_used, ptiles, xs, wgu, bgu, wd, bd)


def _combine_body(dest_ref, dest_next_ref, ys_hbm, h2_ref, gate_ref, nw_ref, o_ref, buf_ref, sems, *, th):
    i = pl.program_id(0)
    half_words = TOP_K * th

    def row_copy(src_row, slot, k, r):
        return pltpu.make_async_copy(ys_hbm.at[src_row], buf_ref.at[slot, k, pl.ds(r, 1)], sems.at[slot])

    def issue(idx_ref, base, slot):
        for r in range(th):
            for k in range(TOP_K):
                row_copy(idx_ref[base + k * th + r], slot, k, r).start(priority=k % DMA_PRIORITIES)

    def drain(slot):
        def body(r, carry):
            for k in range(TOP_K):
                row_copy(0, slot, k, 0).wait()
            return carry
        lax.fori_loop(0, th, body, 0, unroll=8)

    g = gate_ref[...]
    g_cols = jnp.concatenate([g, jnp.zeros((LANES - g.shape[0], 2 * th), F32)], axis=0).T

    def finish(half):
        rows = slice(half * th, (half + 1) * th)
        h = h2_ref[rows, :]
        for k in range(TOP_K):
            h = h + g_cols[rows, k:k + 1] * buf_ref[half, k]
        o_ref[rows, :] = h * lax.rsqrt(jnp.mean(h * h, axis=-1, keepdims=True) + EPS) * nw_ref[...]

    @pl.when(i == 0)
    def _first_half_tile():
        def body(r, carry):
            for k in range(TOP_K):
                row_copy(dest_ref[k * th + r], 0, k, r).start(priority=k % DMA_PRIORITIES)
            return carry
        lax.fori_loop(0, th, body, 0, unroll=8)

    issue(dest_ref, half_words, 1)
    drain(0)
    finish(0)

    @pl.when(i + 1 < pl.num_programs(0))
    def _next_step_first_half_tile():
        issue(dest_next_ref, 0, 0)

    drain(1)
    finish(1)


def _combine(dest_tiles, ys, h2, gates, norm_w, th):
    rows, d = h2.shape
    steps = rows // (2 * th)
    return pl.pallas_call(
        functools.partial(_combine_body, th=th),
        out_shape=jax.ShapeDtypeStruct((rows, d), F32),
        grid=(steps,),
        in_specs=[
            pl.BlockSpec((2 * TOP_K * th,), lambda i: (i,), memory_space=pltpu.SMEM),
            pl.BlockSpec((TOP_K * th,), lambda i: (jnp.minimum(2 * i + 2, 2 * steps - 1),),
                         memory_space=pltpu.SMEM),
            pl.BlockSpec(memory_space=pl.ANY),
            pl.BlockSpec((2 * th, d), lambda i: (i, 0)),
            pl.BlockSpec((8, 2 * th), lambda i: (0, i)),
            pl.BlockSpec((1, d), lambda i: (0, 0)),
        ],
        out_specs=pl.BlockSpec((2 * th, d), lambda i: (i, 0)),
        scratch_shapes=[pltpu.VMEM((2, TOP_K, th, d), F32), pltpu.SemaphoreType.DMA((2,))],
        compiler_params=_params(1),
        name="moe_combine",
    )(dest_tiles, dest_tiles, ys, h2, gates, norm_w)


def _rope_tables(positions):
    inv_freq = ROPE_BASE ** (-np.arange(0, RET_DK, 2, dtype=np.float64) / RET_DK)
    ang = np.asarray(positions, np.float64)[:, None] * inv_freq[None, :]
    return jnp.asarray(np.cos(ang), F32), jnp.asarray(np.sin(ang), F32)


def _retention_tables():
    c = RET_CHUNK
    log_gamma = jnp.log1p(-jnp.exp2(-5.0 - jnp.arange(RET_HEADS, dtype=F32)))
    idx = jnp.arange(c, dtype=F32)
    diff = idx[:, None] - idx[None, :]
    dmat = jnp.where(diff[None] >= 0, jnp.exp(log_gamma[:, None, None] * jnp.maximum(diff, 0.0)[None]), 0.0)
    zeta = jnp.exp(log_gamma[:, None] * (c - 1 - idx)[None, :])[:, :, None]
    xi = jnp.exp(log_gamma[:, None] * (idx + 1.0)[None, :])[:, :, None]
    chunk_decay = jnp.exp(log_gamma * c)
    return chunk_decay, dmat, zeta, xi


def _largest_tile(n, cap):
    t = cap
    while n % t:
        t //= 2
    return t


def _tile_major(a, tile):
    k, t = a.shape
    return a.reshape(k, t // tile, tile).transpose(1, 0, 2).reshape(-1)


def kernel(x, meta_tokens, norm_mix_w, w_in, conv_dw_w, conv_dw_b, conv_ln_w, conv_ln_b, w_conv_out,
           b_conv_out, ret_gn_w, w_ret_out, w_mix_out, norm_ffn_w, w_router, b_router, w_gate_up,
           b_gate_up, w_down, b_down, norm_final_w):
    batch, seq, d = x.shape
    depth = w_in.shape[0]
    n_experts = w_router.shape[-1]
    assert depth == 1 and d == 1024 and seq % RET_CHUNK == 0
    assert w_in.shape[-1] == 10 * 1024 and w_gate_up.shape[-1] == 2 * d
    rows = batch * seq

    tm_in = _largest_tile(seq, 512)
    tt_ret = _largest_tile(seq, 512)
    tt_tail = _largest_tile(seq, 512)
    td = _largest_tile(rows, 512)
    tc_half = _largest_tile(rows, 512) // 2
    tm_e = 512

    x2d = x.reshape(rows, d)
    w_in_bf = w_in[0].astype(BF16)
    row1 = lambda v: v.reshape(1, -1)
    norm_w = row1(norm_mix_w[0])

    cos, sin = _rope_tables(N_META + np.arange(seq))
    u, qk, v, sg, gates = _inproj(x2d, norm_w, w_in_bf, cos, sin, tm_in)

    pad = RET_CHUNK - N_META
    meta_chunk = jnp.concatenate([jnp.zeros((pad, d), x.dtype), meta_tokens.astype(x.dtype)], axis=0)
    cos_m, sin_m = _rope_tables(np.maximum(np.arange(RET_CHUNK) - pad, 0))
    u_meta, qk_meta, v_meta, _, _ = _inproj(meta_chunk, norm_w, w_in_bf, cos_m, sin_m, RET_CHUNK)

    yg = _retention(qk, v, sg, qk_meta, v_meta, _retention_tables(), row1(ret_gn_w[0]), batch, seq, tt_ret)

    tail_params = dict(
        dw_w=conv_dw_w[0], dw_b=row1(conv_dw_b[0]), ln_w=row1(conv_ln_w[0]), ln_b=row1(conv_ln_b[0]),
        w_conv_out=w_conv_out[0].astype(BF16), b_conv_out=row1(b_conv_out[0]),
        w_ret_out=w_ret_out[0].astype(BF16), w_mix_out=w_mix_out[0].astype(BF16),
        norm_ffn_w=row1(norm_ffn_w[0]), w_router_t=w_router[0].T, b_router=b_router[0].reshape(-1, 1))
    h2, xn, topi, gate_w, rank, counts = _tail(x2d, u, gates, yg, u_meta, tail_params, batch, seq, tt_tail,
                                               n_experts)

    counts = counts[:, 0].astype(I32)
    pcounts = (counts + tm_e - 1) // tm_e * tm_e
    pends = jnp.cumsum(pcounts).astype(I32)
    pstarts = pends - pcounts
    experts = jnp.arange(n_experts, dtype=I32)
    start_of = jnp.sum(jnp.where(topi[:TOP_K, :, None] == experts, pstarts, 0), axis=-1)
    dest = start_of + rank[:TOP_K]
    n_tiles = rows * TOP_K // tm_e + n_experts
    tile_starts = jnp.arange(n_tiles, dtype=I32) * tm_e
    tile_e = jnp.minimum(jnp.sum(pends[None, :] <= tile_starts[:, None], axis=-1), n_experts - 1).astype(I32)
    n_used = (pends[-1:] // tm_e).astype(I32)

    xs = _dispatch(pends, pcounts, _tile_major(dest, td), xn, n_tiles * tm_e, td, tm_e, n_experts)
    ys = _experts(tile_e, n_used, pcounts // tm_e, xs, w_gate_up[0], b_gate_up[0][:, None, :],
                  w_down[0], b_down[0][:, None, :], tm_e)
    out = _combine(_tile_major(dest, tc_half), ys, h2, gate_w, row1(norm_final_w), tc_half)
    return out.reshape(batch, seq, d)
```

```python
import functools

import jax
import jax.numpy as jnp
import numpy as np
from jax import lax
from jax.experimental import pallas as pl
from jax.experimental.pallas import tpu as pltpu

N_META = 16
CONV_WIDTH = 31
RET_HEADS = 4
RET_DK = 256
RET_DV = 512
RET_CHUNK = 128
ROPE_BASE = 10000.0
TOP_K = 4
SWIGLU_LIMIT = 7.0
SWIGLU_ALPHA = 1.702
EPS = 1e-5

LANES = 128
SUBLANES = 8
CONV_HALO = 32
DMA_PRIORITIES = 2
VMEM_LIMIT = 56 * 1024 * 1024

F32 = jnp.float32
BF16 = jnp.bfloat16
I32 = jnp.int32

WB_CONV_A, WB_CONV_B, WB_Q, WB_V, WB_G, WB_GATE = 0, 1, 2, 4, 6, 8


def _params(n_grid_axes):
    return pltpu.CompilerParams(dimension_semantics=("arbitrary",) * n_grid_axes,
                                vmem_limit_bytes=VMEM_LIMIT)


def _silu(x):
    return x * jax.nn.sigmoid(x)


def _inproj_body(x_ref, nw_ref, w_ref, cos_ref, sin_ref, u_ref, qk_ref, v_ref, sg_ref, gate_ref):
    x = x_ref[...]
    ms = jnp.mean(x * x, axis=-1, keepdims=True)
    hn = (x * lax.rsqrt(ms + EPS) * nw_ref[...]).astype(BF16)

    def proj(block):
        return jnp.dot(hn, w_ref[:, block * 1024:(block + 1) * 1024], preferred_element_type=F32)

    u_ref[...] = (proj(WB_CONV_A) * jax.nn.sigmoid(proj(WB_CONV_B))).astype(BF16)

    half = RET_DK // 2
    for j, scale in enumerate((RET_DK ** -0.5, 1.0)):
        acc = proj(WB_Q + j)
        cos = cos_ref[...] * scale
        sin = sin_ref[...] * scale
        for h in range(RET_HEADS):
            lo = j * 1024 + h * RET_DK
            x1 = acc[:, h * RET_DK:h * RET_DK + half]
            x2 = acc[:, h * RET_DK + half:(h + 1) * RET_DK]
            qk_ref[:, lo:lo + half] = (x1 * cos - x2 * sin).astype(BF16)
            qk_ref[:, lo + half:lo + RET_DK] = (x2 * cos + x1 * sin).astype(BF16)

    for j in range(2):
        cols = slice(j * 1024, (j + 1) * 1024)
        v_ref[:, cols] = proj(WB_V + j).astype(BF16)
        sg_ref[:, cols] = _silu(proj(WB_G + j)).astype(BF16)
        gate_ref[:, cols] = jax.nn.sigmoid(proj(WB_GATE + j)).astype(BF16)


def _inproj(x2d, norm_w, w_in_bf, cos, sin, tm):
    rows, d = x2d.shape
    period_tiles = cos.shape[0] // tm
    wide = pl.BlockSpec((tm, 2048), lambda i: (i, 0))
    return pl.pallas_call(
        _inproj_body,
        out_shape=(jax.ShapeDtypeStruct((rows, 1024), BF16),) + (jax.ShapeDtypeStruct((rows, 2048), BF16),) * 4,
        grid=(rows // tm,),
        in_specs=[
            pl.BlockSpec((tm, d), lambda i: (i, 0)),
            pl.BlockSpec((1, d), lambda i: (0, 0)),
            pl.BlockSpec(w_in_bf.shape, lambda i: (0, 0), pipeline_mode=pl.Buffered(1)),
            pl.BlockSpec((tm, LANES), lambda i: (i % period_tiles, 0)),
            pl.BlockSpec((tm, LANES), lambda i: (i % period_tiles, 0)),
        ],
        out_specs=(pl.BlockSpec((tm, 1024), lambda i: (i, 0)), wide, wide, wide, wide),
        compiler_params=_params(1),
        name="inproj",
    )(x2d, norm_w, w_in_bf, cos, sin)


def _dot_t0(a, b):
    return lax.dot_general(a, b, (((0,), (0,)), ((), ())), preferred_element_type=F32)


def _dot_t1(a, b):
    return lax.dot_general(a, b, (((1,), (1,)), ((), ())), preferred_element_type=F32)


def _retention_body(cd_ref, q_ref, k_ref, v_ref, sg_ref, km_ref, vm_ref, dmat_ref, zeta_ref, xi_ref, gnw_ref,
                    o_ref, state_ref, *, n_chunks):
    t = pl.program_id(1)
    C, DK, DV = RET_CHUNK, RET_DK, RET_DV

    def state_update(prev, h, k_h, v_h):
        kz = (k_h.astype(F32) * zeta_ref[h]).astype(BF16)
        return prev * cd_ref[h] + _dot_t0(kz, v_h)

    @pl.when(t == 0)
    def _seed_from_meta_chunk():
        for h in range(RET_HEADS):
            state_ref[h] = state_update(jnp.zeros((DK, DV), F32), h,
                                        km_ref[:, h * DK:(h + 1) * DK], vm_ref[:, h * DV:(h + 1) * DV])

    def chunk(c, carry):
        rows = pl.ds(pl.multiple_of(c * C, C), C)
        for h in range(RET_HEADS):
            q_h = q_ref[rows, h * DK:(h + 1) * DK]
            k_h = k_ref[rows, h * DK:(h + 1) * DK]
            v_h = v_ref[rows, h * DV:(h + 1) * DV]
            scores = _dot_t1(q_h, k_h) * dmat_ref[h]
            inner = jnp.dot(scores.astype(BF16), v_h, preferred_element_type=F32)
            state = state_ref[h]
            cross = jnp.dot(q_h, state.astype(BF16), preferred_element_type=F32) * xi_ref[h]
            state_ref[h] = state_update(state, h, k_h, v_h)
            y = inner + cross
            mu = jnp.mean(y, axis=-1, keepdims=True)
            yc = y - mu
            yn = yc * lax.rsqrt(jnp.mean(yc * yc, axis=-1, keepdims=True) + EPS)
            gated = yn * gnw_ref[:, h * DV:(h + 1) * DV] * sg_ref[rows, h * DV:(h + 1) * DV].astype(F32)
            o_ref[rows, h * DV:(h + 1) * DV] = gated.astype(BF16)
        return carry

    lax.fori_loop(0, n_chunks, chunk, 0)


def _retention(qk, v, sg, qk_meta, v_meta, tables, gn_w, batch, seq, tt):
    chunk_decay, dmat, zeta, xi = tables
    tiles = seq // tt
    row = lambda b, t: b * tiles + t
    hv = RET_HEADS * RET_DV
    full = lambda shape: pl.BlockSpec(shape, lambda b, t: (0,) * len(shape))
    return pl.pallas_call(
        functools.partial(_retention_body, n_chunks=tt // RET_CHUNK),
        out_shape=jax.ShapeDtypeStruct((batch * seq, hv), BF16),
        grid=(batch, tiles),
        in_specs=[
            pl.BlockSpec(memory_space=pltpu.SMEM),
            pl.BlockSpec((tt, 1024), lambda b, t: (row(b, t), 0)),
            pl.BlockSpec((tt, 1024), lambda b, t: (row(b, t), 1)),
            pl.BlockSpec((tt, hv), lambda b, t: (row(b, t), 0)),
            pl.BlockSpec((tt, hv), lambda b, t: (row(b, t), 0)),
            pl.BlockSpec((RET_CHUNK, 1024), lambda b, t: (0, 1)),
            full((RET_CHUNK, hv)),
            full((RET_HEADS, RET_CHUNK, RET_CHUNK)),
            full((RET_HEADS, RET_CHUNK, 1)),
            full((RET_HEADS, RET_CHUNK, 1)),
            full((1, hv)),
        ],
        out_specs=pl.BlockSpec((tt, hv), lambda b, t: (row(b, t), 0)),
        scratch_shapes=[pltpu.VMEM((RET_HEADS, RET_DK, RET_DV), F32)],
        compiler_params=_params(2),
        name="retention",
    )(chunk_decay, qk, qk, v, sg, qk_meta, v_meta, dmat, zeta, xi, gn_w)


def _tail_body(x_ref, u_ref, gates_ref, yg_ref, um_ref, dww_ref, dwb_ref, lnw_ref, lnb_ref,
               wco_ref, bco_ref, wro_ref, wmo_ref, nfw_ref, wrt_ref, br_ref, tri_ref,
               h2_ref, xn_ref, topi_ref, gate_ref, rank_ref, cnt_ref,
               uext_ref, phase_ref, conv_ref, count_ref, *, tt, n_experts):
    b = pl.program_id(0)
    t = pl.program_id(1)
    C = RET_CHUNK
    d = x_ref.shape[1]

    @pl.when((b == 0) & (t == 0))
    def _zero_counts():
        count_ref[...] = jnp.zeros_like(count_ref)

    @pl.when(t == 0)
    def _seed_conv_history():
        uext_ref[0:CONV_HALO, :] = um_ref[C - CONV_HALO:C, :].astype(F32)

    uext_ref[CONV_HALO:CONV_HALO + tt, :] = u_ref[...].astype(F32)

    first = CONV_HALO - (CONV_WIDTH - 1)
    for s in range(d // LANES):
        cols = slice(s * LANES, (s + 1) * LANES)
        for phase in range(1, SUBLANES):
            phase_ref[phase - 1] = uext_ref[phase:phase + phase_ref.shape[1], cols]
        for c in range(tt // C):
            acc = jnp.zeros((C, LANES), F32)
            for k in range(CONV_WIDTH):
                a, phase = divmod(first + k, SUBLANES)
                r0 = c * C + a * SUBLANES
                window = uext_ref[r0:r0 + C, cols] if phase == 0 else phase_ref[phase - 1, r0:r0 + C, :]
                acc = acc + dww_ref[k:k + 1, cols] * window
            conv_ref[c * C:(c + 1) * C, cols] = acc + dwb_ref[:, cols]
    uext_ref[0:CONV_HALO, :] = uext_ref[tt:tt + CONV_HALO, :]

    cv = conv_ref[...]
    mu = jnp.mean(cv, axis=-1, keepdims=True)
    cc = cv - mu
    ln = cc * lax.rsqrt(jnp.mean(cc * cc, axis=-1, keepdims=True) + EPS) * lnw_ref[...] + lnb_ref[...]
    act = (ln * jax.nn.sigmoid(ln)).astype(BF16)
    y_conv = jnp.dot(act, wco_ref[...], preferred_element_type=F32) + bco_ref[...]
    y_ret = jnp.dot(yg_ref[...], wro_ref[...], preferred_element_type=F32)
    merged = (gates_ref[:, :d].astype(F32) * y_conv + gates_ref[:, d:].astype(F32) * y_ret).astype(BF16)
    h2 = x_ref[...] + jnp.dot(merged, wmo_ref[...], preferred_element_type=F32)
    h2_ref[...] = h2

    xn = h2 * lax.rsqrt(jnp.mean(h2 * h2, axis=-1, keepdims=True) + EPS) * nfw_ref[...]
    xn_ref[...] = xn

    logits = lax.dot_general(wrt_ref[...], xn, (((1,), (1,)), ((), ())),
                             precision=lax.Precision.HIGHEST, preferred_element_type=F32) + br_ref[...]
    eidx = lax.broadcasted_iota(I32, (n_experts, tt), 0)
    selected = jnp.zeros((n_experts, tt), F32)
    vals, idxs = [], []
    for _ in range(TOP_K):
        m = jnp.max(logits, axis=0, keepdims=True)
        idx = jnp.min(jnp.where(logits == m, eidx, n_experts), axis=0, keepdims=True)
        hit = eidx == idx
        vals.append(m)
        idxs.append(idx)
        selected = jnp.where(hit, 1.0, selected)
        logits = jnp.where(hit, -jnp.inf, logits)
    exps = [jnp.exp(v - vals[0]) for v in vals]
    denom = exps[0] + exps[1] + exps[2] + exps[3]

    before = jnp.dot(selected.astype(BF16), tri_ref[...], preferred_element_type=F32) + count_ref[:, 0:1]
    ranks = [jnp.sum(jnp.where(eidx == idx, before, 0.0), axis=0, keepdims=True).astype(I32) for idx in idxs]
    count_ref[...] = count_ref[...] + jnp.sum(selected, axis=1, keepdims=True)

    zi = jnp.zeros((8 - TOP_K, tt), I32)
    topi_ref[...] = jnp.concatenate(idxs + [zi], axis=0)
    rank_ref[...] = jnp.concatenate(ranks + [zi], axis=0)
    gate_ref[...] = jnp.concatenate([e / denom for e in exps] + [zi.astype(F32)], axis=0)
    cnt_ref[...] = count_ref[...]


def _tail(x2d, u, gates, yg, u_meta, p, batch, seq, tt, n_experts):
    d = x2d.shape[1]
    tiles = seq // tt
    rows = batch * seq
    row = lambda b, t: b * tiles + t
    full = lambda shape: pl.BlockSpec(shape, lambda b, t: (0,) * len(shape), pipeline_mode=pl.Buffered(1))
    tri = (jnp.arange(tt)[:, None] < jnp.arange(tt)[None, :]).astype(BF16)
    lane_rows = pl.BlockSpec((8, tt), lambda b, t: (0, row(b, t)))
    return pl.pallas_call(
        functools.partial(_tail_body, tt=tt, n_experts=n_experts),
        out_shape=(
            jax.ShapeDtypeStruct((rows, d), F32),
            jax.ShapeDtypeStruct((rows, d), F32),
            jax.ShapeDtypeStruct((8, rows), I32),
            jax.ShapeDtypeStruct((8, rows), F32),
            jax.ShapeDtypeStruct((8, rows), I32),
            jax.ShapeDtypeStruct((n_experts, LANES), F32),
        ),
        grid=(batch, tiles),
        in_specs=[
            pl.BlockSpec((tt, d), lambda b, t: (row(b, t), 0)),
            pl.BlockSpec((tt, 1024), lambda b, t: (row(b, t), 0)),
            pl.BlockSpec((tt, 2048), lambda b, t: (row(b, t), 0)),
            pl.BlockSpec((tt, yg.shape[1]), lambda b, t: (row(b, t), 0)),
            full((RET_CHUNK, 1024)),
            full((CONV_WIDTH, d)), full((1, d)), full((1, d)), full((1, d)),
            full((d, d)), full((1, d)),
            full((yg.shape[1], d)),
            full((d, d)),
            full((1, d)),
            full((n_experts, d)), full((n_experts, 1)),
            full((tt, tt)),
        ],
        out_specs=(
            pl.BlockSpec((tt, d), lambda b, t: (row(b, t), 0)),
            pl.BlockSpec((tt, d), lambda b, t: (row(b, t), 0)),
            lane_rows, lane_rows, lane_rows,
            pl.BlockSpec((n_experts, LANES), lambda b, t: (0, 0)),
        ),
        scratch_shapes=[
            pltpu.VMEM((CONV_HALO + tt, d), F32),
            pltpu.VMEM((SUBLANES - 1, CONV_HALO + tt - SUBLANES, LANES), F32),
            pltpu.VMEM((tt, d), F32),
            pltpu.VMEM((n_experts, LANES), F32),
        ],
        compiler_params=_params(2),
        name="mixer_tail",
    )(x2d, u, gates, yg, u_meta, p["dw_w"], p["dw_b"], p["ln_w"], p["ln_b"], p["w_conv_out"], p["b_conv_out"],
      p["w_ret_out"], p["w_mix_out"], p["norm_ffn_w"], p["w_router_t"], p["b_router"], tri)


def _dispatch_body(pend_ref, pcnt_ref, dest_ref, xn_ref, xs_hbm, zero_ref, sem, zsem, *, td, tm, n_experts):
    def zero_copy(e):
        start = pl.multiple_of(pend_ref[e] - tm, tm)
        return pltpu.make_async_copy(zero_ref, xs_hbm.at[pl.ds(start, tm)], zsem)

    @pl.when(pl.program_id(0) == 0)
    def _zero_last_tile_of_every_expert():
        zero_ref[...] = jnp.zeros_like(zero_ref)
        for e in range(n_experts):
            pl.when(pcnt_ref[e] > 0)(lambda e=e: zero_copy(e).start())
        for e in range(n_experts):
            pl.when(pcnt_ref[e] > 0)(lambda e=e: zero_copy(e).wait())

    def row_copy(r, dst_row):
        return pltpu.make_async_copy(xn_ref.at[pl.ds(r, 1)], xs_hbm.at[pl.ds(dst_row, 1)], sem)

    for r in range(td):
        for k in range(TOP_K):
            row_copy(r, dest_ref[k * td + r]).start(priority=k % DMA_PRIORITIES)

    def drain(r, carry):
        for k in range(TOP_K):
            row_copy(0, 0).wait()
        return carry

    lax.fori_loop(0, td, drain, 0, unroll=8)


def _dispatch(pends, pcounts, dest_tiles, xn, n_rows_out, td, tm, n_experts):
    rows, w = xn.shape
    return pl.pallas_call(
        functools.partial(_dispatch_body, td=td, tm=tm, n_experts=n_experts),
        out_shape=jax.ShapeDtypeStruct((n_rows_out, w), F32),
        grid_spec=pltpu.PrefetchScalarGridSpec(
            num_scalar_prefetch=2,
            grid=(rows // td,),
            in_specs=[
                pl.BlockSpec((TOP_K * td,), lambda i, *_: (i,), memory_space=pltpu.SMEM),
                pl.BlockSpec((td, w), lambda i, *_: (i, 0)),
            ],
            out_specs=pl.BlockSpec(memory_space=pl.ANY),
            scratch_shapes=[pltpu.VMEM((tm, w), F32), pltpu.SemaphoreType.DMA, pltpu.SemaphoreType.DMA],
        ),
        compiler_params=_params(1),
        name="moe_dispatch",
    )(pends, pcounts, dest_tiles, xn)


def _experts_body(tile_e_ref, n_used_ref, ptiles_ref, xs_ref, wgu_hbm, bgu_ref, wd_hbm, bd_ref, ys_ref,
                  wgu_stage_ref, wd_stage_ref, wgu_bf_ref, wd_bf_ref, sems):
    p = pl.program_id(0)
    n_used = n_used_ref[0]

    def weight_copies(e):
        return (pltpu.make_async_copy(wgu_hbm.at[e], wgu_stage_ref, sems.at[0]),
                pltpu.make_async_copy(wd_hbm.at[e], wd_stage_ref, sems.at[1]))

    @pl.when(p < n_used)
    def _():
        e = tile_e_ref[p]

        @pl.when(p == 0)
        def _fetch_first_expert():
            for c in weight_copies(e):
                c.start()

        @pl.when((p == 0) | (e != tile_e_ref[jnp.maximum(p - 1, 0)]))
        def _switch_expert():
            for c in weight_copies(e):
                c.wait()
            wgu_bf_ref[...] = wgu_stage_ref[...].astype(BF16)
            wd_bf_ref[...] = wd_stage_ref[...].astype(BF16)
            nxt = p + ptiles_ref[e]

            @pl.when(nxt < n_used)
            def _fetch_next_expert():
                for c in weight_copies(tile_e_ref[nxt]):
                    c.start()

        x = xs_ref[...].astype(BF16)
        hgu = jnp.dot(x, wgu_bf_ref[...], preferred_element_type=F32) + bgu_ref[0]
        de = hgu.shape[1] // 2
        h_gate = jnp.minimum(hgu[:, :de], SWIGLU_LIMIT)
        h_up = jnp.clip(hgu[:, de:], -SWIGLU_LIMIT, SWIGLU_LIMIT)
        act = h_gate * jax.nn.sigmoid(SWIGLU_ALPHA * h_gate) * (h_up + 1.0)
        y = jnp.dot(act.astype(BF16), wd_bf_ref[...], preferred_element_type=F32) + bd_ref[0]
        ys_ref[...] = y


def _experts(tile_e, n_used, ptiles, xs, wgu, bgu, wd, bd, tm):
    n_rows, w = xs.shape
    e, d, de2 = wgu.shape
    tile = lambda p, te, nu, pt: jnp.minimum(p, nu[0] - 1)
    expert = lambda p, te, nu, pt: te[tile(p, te, nu, pt)]
    return pl.pallas_call(
        _experts_body,
        out_shape=jax.ShapeDtypeStruct((n_rows, w), F32),
        grid_spec=pltpu.PrefetchScalarGridSpec(
            num_scalar_prefetch=3,
            grid=(n_rows // tm,),
            in_specs=[
                pl.BlockSpec((tm, w), lambda p, te, nu, pt: (tile(p, te, nu, pt), 0)),
                pl.BlockSpec(memory_space=pl.ANY),
                pl.BlockSpec((1, 1, de2), lambda p, te, nu, pt: (expert(p, te, nu, pt), 0, 0)),
                pl.BlockSpec(memory_space=pl.ANY),
                pl.BlockSpec((1, 1, d), lambda p, te, nu, pt: (expert(p, te, nu, pt), 0, 0)),
            ],
            out_specs=pl.BlockSpec((tm, w), lambda p, te, nu, pt: (tile(p, te, nu, pt), 0)),
            scratch_shapes=[pltpu.VMEM((d, de2), F32), pltpu.VMEM((de2 // 2, d), F32),
                            pltpu.VMEM((d, de2), BF16), pltpu.VMEM((de2 // 2, d), BF16),
                            pltpu.SemaphoreType.DMA((2,))],
        ),
        compiler_params=_params(1),
        name="moe_experts",
    )(tile_e, n_used, ptiles, xs, wgu, bgu, wd, bd)


def _combine_body(dest_ref, dest_next_ref, ys_hbm, h2_ref, gate_ref, nw_ref, o_ref, buf_ref, sems, *, th):
    i = pl.program_id(0)
    half_words = TOP_K * th

    def row_copy(src_row, slot, k, r):
        return pltpu.make_async_copy(ys_hbm.at[pl.ds(src_row, 1)], buf_ref.at[slot, k, pl.ds(r, 1)],
                                     sems.at[slot])

    def issue(idx_ref, base, slot):
        for r in range(th):
            for k in range(TOP_K):
                row_copy(idx_ref[base + k * th + r], slot, k, r).start(priority=k % DMA_PRIORITIES)

    def drain(slot):
        def body(r, carry):
            for k in range(TOP_K):
                row_copy(0, slot, k, 0).wait()
            return carry
        lax.fori_loop(0, th, body, 0, unroll=8)

    g = gate_ref[...]
    g_cols = jnp.concatenate([g, jnp.zeros((LANES - g.shape[0], 2 * th), F32)], axis=0).T

    def finish(half):
        rows = slice(half * th, (half + 1) * th)
        h = h2_ref[rows, :]
        for k in range(TOP_K):
            h = h + g_cols[rows, k:k + 1] * buf_ref[half, k]
        o_ref[rows, :] = h * lax.rsqrt(jnp.mean(h * h, axis=-1, keepdims=True) + EPS) * nw_ref[...]

    @pl.when(i == 0)
    def _first_half_tile():
        def body(r, carry):
            for k in range(TOP_K):
                row_copy(dest_ref[k * th + r], 0, k, r).start(priority=k % DMA_PRIORITIES)
            return carry
        lax.fori_loop(0, th, body, 0, unroll=8)

    issue(dest_ref, half_words, 1)
    drain(0)
    finish(0)

    @pl.when(i + 1 < pl.num_programs(0))
    def _next_step_first_half_tile():
        issue(dest_next_ref, 0, 0)

    drain(1)
    finish(1)


def _combine(dest_tiles, ys, h2, gates, norm_w, th):
    rows, d = h2.shape
    steps = rows // (2 * th)
    return pl.pallas_call(
        functools.partial(_combine_body, th=th),
        out_shape=jax.ShapeDtypeStruct((rows, d), F32),
        grid=(steps,),
        in_specs=[
            pl.BlockSpec((2 * TOP_K * th,), lambda i: (i,), memory_space=pltpu.SMEM),
            pl.BlockSpec((TOP_K * th,), lambda i: (jnp.minimum(2 * i + 2, 2 * steps - 1),),
                         memory_space=pltpu.SMEM),
            pl.BlockSpec(memory_space=pl.ANY),
            pl.BlockSpec((2 * th, d), lambda i: (i, 0)),
            pl.BlockSpec((8, 2 * th), lambda i: (0, i)),
            pl.BlockSpec((1, d), lambda i: (0, 0)),
        ],
        out_specs=pl.BlockSpec((2 * th, d), lambda i: (i, 0)),
        scratch_shapes=[pltpu.VMEM((2, TOP_K, th, d), F32), pltpu.SemaphoreType.DMA((2,))],
        compiler_params=_params(1),
        name="moe_combine",
    )(dest_tiles, dest_tiles, ys, h2, gates, norm_w)


def _rope_tables(positions):
    inv_freq = ROPE_BASE ** (-np.arange(0, RET_DK, 2, dtype=np.float64) / RET_DK)
    ang = np.asarray(positions, np.float64)[:, None] * inv_freq[None, :]
    return jnp.asarray(np.cos(ang), F32), jnp.asarray(np.sin(ang), F32)


def _retention_tables():
    c = RET_CHUNK
    log_gamma = jnp.log1p(-jnp.exp2(-5.0 - jnp.arange(RET_HEADS, dtype=F32)))
    idx = jnp.arange(c, dtype=F32)
    diff = idx[:, None] - idx[None, :]
    dmat = jnp.where(diff[None] >= 0, jnp.exp(log_gamma[:, None, None] * jnp.maximum(diff, 0.0)[None]), 0.0)
    zeta = jnp.exp(log_gamma[:, None] * (c - 1 - idx)[None, :])[:, :, None]
    xi = jnp.exp(log_gamma[:, None] * (idx + 1.0)[None, :])[:, :, None]
    chunk_decay = jnp.exp(log_gamma * c)
    return chunk_decay, dmat, zeta, xi


def _largest_tile(n, cap):
    t = cap
    while n % t:
        t //= 2
    return t


def _tile_major(a, tile):
    k, t = a.shape
    return a.reshape(k, t // tile, tile).transpose(1, 0, 2).reshape(-1)


def kernel(x, meta_tokens, norm_mix_w, w_in, conv_dw_w, conv_dw_b, conv_ln_w, conv_ln_b, w_conv_out,
           b_conv_out, ret_gn_w, w_ret_out, w_mix_out, norm_ffn_w, w_router, b_router, w_gate_up,
           b_gate_up, w_down, b_down, norm_final_w):
    batch, seq, d = x.shape
    depth = w_in.shape[0]
    n_experts = w_router.shape[-1]
    assert depth == 1 and d == 1024 and seq % RET_CHUNK == 0
    assert w_in.shape[-1] == 10 * 1024 and w_gate_up.shape[-1] == 2 * d
    rows = batch * seq

    tm_in = _largest_tile(seq, 512)
    tt_ret = _largest_tile(seq, 512)
    tt_tail = _largest_tile(seq, 512)
    td = _largest_tile(rows, 512)
    tc_half = _largest_tile(rows, 512) // 2
    tm_e = 512

    x2d = x.reshape(rows, d)
    w_in_bf = w_in[0].astype(BF16)
    row1 = lambda v: v.reshape(1, -1)
    norm_w = row1(norm_mix_w[0])

    cos, sin = _rope_tables(N_META + np.arange(seq))
    u, qk, v, sg, gates = _inproj(x2d, norm_w, w_in_bf, cos, sin, tm_in)

    pad = RET_CHUNK - N_META
    meta_chunk = jnp.concatenate([jnp.zeros((pad, d), x.dtype), meta_tokens.astype(x.dtype)], axis=0)
    cos_m, sin_m = _rope_tables(np.maximum(np.arange(RET_CHUNK) - pad, 0))
    u_meta, qk_meta, v_meta, _, _ = _inproj(meta_chunk, norm_w, w_in_bf, cos_m, sin_m, RET_CHUNK)

    yg = _retention(qk, v, sg, qk_meta, v_meta, _retention_tables(), row1(ret_gn_w[0]), batch, seq, tt_ret)

    tail_params = dict(
        dw_w=conv_dw_w[0], dw_b=row1(conv_dw_b[0]), ln_w=row1(conv_ln_w[0]), ln_b=row1(conv_ln_b[0]),
        w_conv_out=w_conv_out[0].astype(BF16), b_conv_out=row1(b_conv_out[0]),
        w_ret_out=w_ret_out[0].astype(BF16), w_mix_out=w_mix_out[0].astype(BF16),
        norm_ffn_w=row1(norm_ffn_w[0]), w_router_t=w_router[0].T, b_router=b_router[0].reshape(-1, 1))
    h2, xn, topi, gate_w, rank, counts = _tail(x2d, u, gates, yg, u_meta, tail_params, batch, seq, tt_tail,
                                               n_experts)

    counts = counts[:, 0].astype(I32)
    pcounts = (counts + tm_e - 1) // tm_e * tm_e
    pends = jnp.cumsum(pcounts).astype(I32)
    pstarts = pends - pcounts
    experts = jnp.arange(n_experts, dtype=I32)
    start_of = jnp.sum(jnp.where(topi[:TOP_K, :, None] == experts, pstarts, 0), axis=-1)
    dest = start_of + rank[:TOP_K]
    n_tiles = rows * TOP_K // tm_e + n_experts
    tile_starts = jnp.arange(n_tiles, dtype=I32) * tm_e
    tile_e = jnp.minimum(jnp.sum(pends[None, :] <= tile_starts[:, None], axis=-1), n_experts - 1).astype(I32)
    n_used = (pends[-1:] // tm_e).astype(I32)

    xs = _dispatch(pends, pcounts, _tile_major(dest, td), xn, n_tiles * tm_e, td, tm_e, n_experts)
    ys = _experts(tile_e, n_used, pcounts // tm_e, xs, w_gate_up[0], b_gate_up[0][:, None, :],
                  w_down[0], b_down[0][:, None, :], tm_e)
    out = _combine(_tile_major(dest, tc_half), ys, h2, gate_w, row1(norm_final_w), tc_half)
    return out.reshape(batch, seq, d)
```

```python
import functools

import jax
import jax.numpy as jnp
import numpy as np
from jax import lax
from jax.experimental import pallas as pl
from jax.experimental.pallas import tpu as pltpu

N_META = 16
CONV_WIDTH = 31
RET_HEADS = 4
RET_DK = 256
RET_DV = 512
RET_CHUNK = 128
ROPE_BASE = 10000.0
TOP_K = 4
SWIGLU_LIMIT = 7.0
SWIGLU_ALPHA = 1.702
EPS = 1e-5

LANES = 128
SUBLANES = 8
CONV_HALO = 32
DMA_PRIORITIES = 2
VMEM_LIMIT = 56 * 1024 * 1024

F32 = jnp.float32
BF16 = jnp.bfloat16
I32 = jnp.int32

WB_CONV_A, WB_CONV_B, WB_Q, WB_V, WB_G, WB_GATE = 0, 1, 2, 4, 6, 8


def _params(n_grid_axes):
    return pltpu.CompilerParams(dimension_semantics=("arbitrary",) * n_grid_axes,
                                vmem_limit_bytes=VMEM_LIMIT)


def _silu(x):
    return x * jax.nn.sigmoid(x)


def _inproj_body(x_ref, nw_ref, w_ref, cos_ref, sin_ref, u_ref, qk_ref, v_ref, sg_ref, gate_ref):
    x = x_ref[...]
    ms = jnp.mean(x * x, axis=-1, keepdims=True)
    hn = (x * lax.rsqrt(ms + EPS) * nw_ref[...]).astype(BF16)

    def proj(block):
        return jnp.dot(hn, w_ref[:, block * 1024:(block + 1) * 1024], preferred_element_type=F32)

    u_ref[...] = (proj(WB_CONV_A) * jax.nn.sigmoid(proj(WB_CONV_B))).astype(BF16)

    half = RET_DK // 2
    for j, scale in enumerate((RET_DK ** -0.5, 1.0)):
        acc = proj(WB_Q + j)
        cos = cos_ref[...] * scale
        sin = sin_ref[...] * scale
        for h in range(RET_HEADS):
            lo = j * 1024 + h * RET_DK
            x1 = acc[:, h * RET_DK:h * RET_DK + half]
            x2 = acc[:, h * RET_DK + half:(h + 1) * RET_DK]
            qk_ref[:, lo:lo + half] = (x1 * cos - x2 * sin).astype(BF16)
            qk_ref[:, lo + half:lo + RET_DK] = (x2 * cos + x1 * sin).astype(BF16)

    for j in range(2):
        cols = slice(j * 1024, (j + 1) * 1024)
        v_ref[:, cols] = proj(WB_V + j).astype(BF16)
        sg_ref[:, cols] = _silu(proj(WB_G + j)).astype(BF16)
        gate_ref[:, cols] = jax.nn.sigmoid(proj(WB_GATE + j)).astype(BF16)


def _inproj(x2d, norm_w, w_in_bf, cos, sin, tm):
    rows, d = x2d.shape
    period_tiles = cos.shape[0] // tm
    wide = pl.BlockSpec((tm, 2048), lambda i: (i, 0))
    return pl.pallas_call(
        _inproj_body,
        out_shape=(jax.ShapeDtypeStruct((rows, 1024), BF16),) + (jax.ShapeDtypeStruct((rows, 2048), BF16),) * 4,
        grid=(rows // tm,),
        in_specs=[
            pl.BlockSpec((tm, d), lambda i: (i, 0)),
            pl.BlockSpec((1, d), lambda i: (0, 0)),
            pl.BlockSpec(w_in_bf.shape, lambda i: (0, 0), pipeline_mode=pl.Buffered(1)),
            pl.BlockSpec((tm, LANES), lambda i: (i % period_tiles, 0)),
            pl.BlockSpec((tm, LANES), lambda i: (i % period_tiles, 0)),
        ],
        out_specs=(pl.BlockSpec((tm, 1024), lambda i: (i, 0)), wide, wide, wide, wide),
        compiler_params=_params(1),
        name="inproj",
    )(x2d, norm_w, w_in_bf, cos, sin)


def _dot_t0(a, b):
    return lax.dot_general(a, b, (((0,), (0,)), ((), ())), preferred_element_type=F32)


def _dot_t1(a, b):
    return lax.dot_general(a, b, (((1,), (1,)), ((), ())), preferred_element_type=F32)


def _retention_body(cd_ref, q_ref, k_ref, v_ref, sg_ref, km_ref, vm_ref, dmat_ref, zeta_ref, xi_ref, gnw_ref,
                    o_ref, state_ref, *, n_chunks):
    t = pl.program_id(1)
    C, DK, DV = RET_CHUNK, RET_DK, RET_DV

    def state_update(prev, h, k_h, v_h):
        kz = (k_h.astype(F32) * zeta_ref[h]).astype(BF16)
        return prev * cd_ref[h] + _dot_t0(kz, v_h)

    @pl.when(t == 0)
    def _seed_from_meta_chunk():
        for h in range(RET_HEADS):
            state_ref[h] = state_update(jnp.zeros((DK, DV), F32), h,
                                        km_ref[:, h * DK:(h + 1) * DK], vm_ref[:, h * DV:(h + 1) * DV])

    def chunk(c, carry):
        rows = pl.ds(pl.multiple_of(c * C, C), C)
        for h in range(RET_HEADS):
            q_h = q_ref[rows, h * DK:(h + 1) * DK]
            k_h = k_ref[rows, h * DK:(h + 1) * DK]
            v_h = v_ref[rows, h * DV:(h + 1) * DV]
            scores = _dot_t1(q_h, k_h) * dmat_ref[h]
            inner = jnp.dot(scores.astype(BF16), v_h, preferred_element_type=F32)
            state = state_ref[h]
            cross = jnp.dot(q_h, state.astype(BF16), preferred_element_type=F32) * xi_ref[h]
            state_ref[h] = state_update(state, h, k_h, v_h)
            y = inner + cross
            mu = jnp.mean(y, axis=-1, keepdims=True)
            yc = y - mu
            yn = yc * lax.rsqrt(jnp.mean(yc * yc, axis=-1, keepdims=True) + EPS)
            gated = yn * gnw_ref[:, h * DV:(h + 1) * DV] * sg_ref[rows, h * DV:(h + 1) * DV].astype(F32)
            o_ref[rows, h * DV:(h + 1) * DV] = gated.astype(BF16)
        return carry

    lax.fori_loop(0, n_chunks, chunk, 0)


def _retention(qk, v, sg, qk_meta, v_meta, tables, gn_w, batch, seq, tt):
    chunk_decay, dmat, zeta, xi = tables
    tiles = seq // tt
    row = lambda b, t: b * tiles + t
    hv = RET_HEADS * RET_DV
    full = lambda shape: pl.BlockSpec(shape, lambda b, t: (0,) * len(shape))
    return pl.pallas_call(
        functools.partial(_retention_body, n_chunks=tt // RET_CHUNK),
        out_shape=jax.ShapeDtypeStruct((batch * seq, hv), BF16),
        grid=(batch, tiles),
        in_specs=[
            pl.BlockSpec(memory_space=pltpu.SMEM),
            pl.BlockSpec((tt, 1024), lambda b, t: (row(b, t), 0)),
            pl.BlockSpec((tt, 1024), lambda b, t: (row(b, t), 1)),
            pl.BlockSpec((tt, hv), lambda b, t: (row(b, t), 0)),
            pl.BlockSpec((tt, hv), lambda b, t: (row(b, t), 0)),
            pl.BlockSpec((RET_CHUNK, 1024), lambda b, t: (0, 1)),
            full((RET_CHUNK, hv)),
            full((RET_HEADS, RET_CHUNK, RET_CHUNK)),
            full((RET_HEADS, RET_CHUNK, 1)),
            full((RET_HEADS, RET_CHUNK, 1)),
            full((1, hv)),
        ],
        out_specs=pl.BlockSpec((tt, hv), lambda b, t: (row(b, t), 0)),
        scratch_shapes=[pltpu.VMEM((RET_HEADS, RET_DK, RET_DV), F32)],
        compiler_params=_params(2),
        name="retention",
    )(chunk_decay, qk, qk, v, sg, qk_meta, v_meta, dmat, zeta, xi, gn_w)


def _tail_body(x_ref, u_ref, gates_ref, yg_ref, um_ref, dww_ref, dwb_ref, lnw_ref, lnb_ref,
               wco_ref, bco_ref, wro_ref, wmo_ref, nfw_ref, wrt_ref, br_ref, tri_ref,
               h2_ref, xn_ref, topi_ref, gate_ref, rank_ref, cnt_ref,
               uext_ref, phase_ref, conv_ref, count_ref, *, tt, n_experts):
    b = pl.program_id(0)
    t = pl.program_id(1)
    C = RET_CHUNK
    d = x_ref.shape[1]

    @pl.when((b == 0) & (t == 0))
    def _zero_counts():
        count_ref[...] = jnp.zeros_like(count_ref)

    @pl.when(t == 0)
    def _seed_conv_history():
        uext_ref[0:CONV_HALO, :] = um_ref[C - CONV_HALO:C, :].astype(F32)

    uext_ref[CONV_HALO:CONV_HALO + tt, :] = u_ref[...].astype(F32)

    first = CONV_HALO - (CONV_WIDTH - 1)
    for s in range(d // LANES):
        cols = slice(s * LANES, (s + 1) * LANES)
        for phase in range(1, SUBLANES):
            phase_ref[phase - 1] = uext_ref[phase:phase + phase_ref.shape[1], cols]
        for c in range(tt // C):
            acc = jnp.zeros((C, LANES), F32)
            for k in range(CONV_WIDTH):
                a, phase = divmod(first + k, SUBLANES)
                r0 = c * C + a * SUBLANES
                window = uext_ref[r0:r0 + C, cols] if phase == 0 else phase_ref[phase - 1, r0:r0 + C, :]
                acc = acc + dww_ref[k:k + 1, cols] * window
            conv_ref[c * C:(c + 1) * C, cols] = acc + dwb_ref[:, cols]
    uext_ref[0:CONV_HALO, :] = uext_ref[tt:tt + CONV_HALO, :]

    cv = conv_ref[...]
    mu = jnp.mean(cv, axis=-1, keepdims=True)
    cc = cv - mu
    ln = cc * lax.rsqrt(jnp.mean(cc * cc, axis=-1, keepdims=True) + EPS) * lnw_ref[...] + lnb_ref[...]
    act = (ln * jax.nn.sigmoid(ln)).astype(BF16)
    y_conv = jnp.dot(act, wco_ref[...], preferred_element_type=F32) + bco_ref[...]
    y_ret = jnp.dot(yg_ref[...], wro_ref[...], preferred_element_type=F32)
    merged = (gates_ref[:, :d].astype(F32) * y_conv + gates_ref[:, d:].astype(F32) * y_ret).astype(BF16)
    h2 = x_ref[...] + jnp.dot(merged, wmo_ref[...], preferred_element_type=F32)
    h2_ref[...] = h2

    xn = h2 * lax.rsqrt(jnp.mean(h2 * h2, axis=-1, keepdims=True) + EPS) * nfw_ref[...]
    xn_ref[...] = xn

    logits = lax.dot_general(wrt_ref[...], xn, (((1,), (1,)), ((), ())),
                             precision=lax.Precision.HIGHEST, preferred_element_type=F32) + br_ref[...]
    eidx = lax.broadcasted_iota(I32, (n_experts, tt), 0)
    selected = jnp.zeros((n_experts, tt), F32)
    vals, idxs = [], []
    for _ in range(TOP_K):
        m = jnp.max(logits, axis=0, keepdims=True)
        idx = jnp.min(jnp.where(logits == m, eidx, n_experts), axis=0, keepdims=True)
        hit = eidx == idx
        vals.append(m)
        idxs.append(idx)
        selected = jnp.where(hit, 1.0, selected)
        logits = jnp.where(hit, -jnp.inf, logits)
    exps = [jnp.exp(v - vals[0]) for v in vals]
    denom = exps[0] + exps[1] + exps[2] + exps[3]

    before = jnp.dot(selected.astype(BF16), tri_ref[...], preferred_element_type=F32) + count_ref[:, 0:1]
    ranks = [jnp.sum(jnp.where(eidx == idx, before, 0.0), axis=0, keepdims=True).astype(I32) for idx in idxs]
    count_ref[...] = count_ref[...] + jnp.sum(selected, axis=1, keepdims=True)

    zi = jnp.zeros((8 - TOP_K, tt), I32)
    topi_ref[...] = jnp.concatenate(idxs + [zi], axis=0)
    rank_ref[...] = jnp.concatenate(ranks + [zi], axis=0)
    gate_ref[...] = jnp.concatenate([e / denom for e in exps] + [zi.astype(F32)], axis=0)
    cnt_ref[...] = count_ref[...]


def _tail(x2d, u, gates, yg, u_meta, p, batch, seq, tt, n_experts):
    d = x2d.shape[1]
    tiles = seq // tt
    rows = batch * seq
    row = lambda b, t: b * tiles + t
    full = lambda shape: pl.BlockSpec(shape, lambda b, t: (0,) * len(shape), pipeline_mode=pl.Buffered(1))
    tri = (jnp.arange(tt)[:, None] < jnp.arange(tt)[None, :]).astype(BF16)
    lane_rows = pl.BlockSpec((8, tt), lambda b, t: (0, row(b, t)))
    return pl.pallas_call(
        functools.partial(_tail_body, tt=tt, n_experts=n_experts),
        out_shape=(
            jax.ShapeDtypeStruct((rows, d), F32),
            jax.ShapeDtypeStruct((rows, d), F32),
            jax.ShapeDtypeStruct((8, rows), I32),
            jax.ShapeDtypeStruct((8, rows), F32),
            jax.ShapeDtypeStruct((8, rows), I32),
            jax.ShapeDtypeStruct((n_experts, LANES), F32),
        ),
        grid=(batch, tiles),
        in_specs=[
            pl.BlockSpec((tt, d), lambda b, t: (row(b, t), 0)),
            pl.BlockSpec((tt, 1024), lambda b, t: (row(b, t), 0)),
            pl.BlockSpec((tt, 2048), lambda b, t: (row(b, t), 0)),
            pl.BlockSpec((tt, yg.shape[1]), lambda b, t: (row(b, t), 0)),
            full((RET_CHUNK, 1024)),
            full((CONV_WIDTH, d)), full((1, d)), full((1, d)), full((1, d)),
            full((d, d)), full((1, d)),
            full((yg.shape[1], d)),
            full((d, d)),
            full((1, d)),
            full((n_experts, d)), full((n_experts, 1)),
            full((tt, tt)),
        ],
        out_specs=(
            pl.BlockSpec((tt, d), lambda b, t: (row(b, t), 0)),
            pl.BlockSpec((tt, d), lambda b, t: (row(b, t), 0)),
            lane_rows, lane_rows, lane_rows,
            pl.BlockSpec((n_experts, LANES), lambda b, t: (0, 0)),
        ),
        scratch_shapes=[
            pltpu.VMEM((CONV_HALO + tt, d), F32),
            pltpu.VMEM((SUBLANES - 1, CONV_HALO + tt - SUBLANES, LANES), F32),
            pltpu.VMEM((tt, d), F32),
            pltpu.VMEM((n_experts, LANES), F32),
        ],
        compiler_params=_params(2),
        name="mixer_tail",
    )(x2d, u, gates, yg, u_meta, p["dw_w"], p["dw_b"], p["ln_w"], p["ln_b"], p["w_conv_out"], p["b_conv_out"],
      p["w_ret_out"], p["w_mix_out"], p["norm_ffn_w"], p["w_router_t"], p["b_router"], tri)


def _dispatch_body(pend_ref, pcnt_ref, dest_ref, xn_ref, xs_hbm, zero_ref, sem, zsem, *, td, tm, n_experts):
    def zero_copy(e):
        start = pl.multiple_of(pend_ref[e] - tm, tm)
        return pltpu.make_async_copy(zero_ref, xs_hbm.at[pl.ds(start, tm)], zsem)

    @pl.when(pl.program_id(0) == 0)
    def _zero_last_tile_of_every_expert():
        zero_ref[...] = jnp.zeros_like(zero_ref)
        for e in range(n_experts):
            pl.when(pcnt_ref[e] > 0)(lambda e=e: zero_copy(e).start())
        for e in range(n_experts):
            pl.when(pcnt_ref[e] > 0)(lambda e=e: zero_copy(e).wait())

    def row_copy(r, dst_row):
        return pltpu.make_async_copy(xn_ref.at[pl.ds(r, 1)], xs_hbm.at[pl.ds(dst_row, 1)], sem)

    for r in range(td):
        for k in range(TOP_K):
            row_copy(r, dest_ref[k * td + r]).start(priority=k % DMA_PRIORITIES)

    def drain(r, carry):
        for k in range(TOP_K):
            row_copy(0, 0).wait()
        return carry

    lax.fori_loop(0, td, drain, 0, unroll=8)


def _dispatch(pends, pcounts, dest_tiles, xn, n_rows_out, td, tm, n_experts):
    rows, w = xn.shape
    return pl.pallas_call(
        functools.partial(_dispatch_body, td=td, tm=tm, n_experts=n_experts),
        out_shape=jax.ShapeDtypeStruct((n_rows_out, w), F32),
        grid_spec=pltpu.PrefetchScalarGridSpec(
            num_scalar_prefetch=2,
            grid=(rows // td,),
            in_specs=[
                pl.BlockSpec((TOP_K * td,), lambda i, *_: (i,), memory_space=pltpu.SMEM),
                pl.BlockSpec((td, w), lambda i, *_: (i, 0)),
            ],
            out_specs=pl.BlockSpec(memory_space=pl.ANY),
            scratch_shapes=[pltpu.VMEM((tm, w), F32), pltpu.SemaphoreType.DMA, pltpu.SemaphoreType.DMA],
        ),
        compiler_params=_params(1),
        name="moe_dispatch",
    )(pends, pcounts, dest_tiles, xn)


def _experts_body(tile_e_ref, n_used_ref, ptiles_ref, xs_ref, wgu_hbm, bgu_ref, wd_hbm, bd_ref, ys_ref,
                  wgu_stage_ref, wd_stage_ref, wgu_bf_ref, wd_bf_ref, sems):
    p = pl.program_id(0)
    n_used = n_used_ref[0]

    def weight_copies(e):
        return (pltpu.make_async_copy(wgu_hbm.at[e], wgu_stage_ref, sems.at[0]),
                pltpu.make_async_copy(wd_hbm.at[e], wd_stage_ref, sems.at[1]))

    @pl.when(p < n_used)
    def _():
        e = tile_e_ref[p]

        @pl.when(p == 0)
        def _fetch_first_expert():
            for c in weight_copies(e):
                c.start()

        @pl.when((p == 0) | (e != tile_e_ref[jnp.maximum(p - 1, 0)]))
        def _switch_expert():
            for c in weight_copies(e):
                c.wait()
            wgu_bf_ref[...] = wgu_stage_ref[...].astype(BF16)
            wd_bf_ref[...] = wd_stage_ref[...].astype(BF16)
            nxt = p + ptiles_ref[e]

            @pl.when(nxt < n_used)
            def _fetch_next_expert():
                for c in weight_copies(tile_e_ref[nxt]):
                    c.start()

        x = xs_ref[...].astype(BF16)
        hgu = jnp.dot(x, wgu_bf_ref[...], preferred_element_type=F32) + bgu_ref[0]
        de = hgu.shape[1] // 2
        h_gate = jnp.minimum(hgu[:, :de], SWIGLU_LIMIT)
        h_up = jnp.clip(hgu[:, de:], -SWIGLU_LIMIT, SWIGLU_LIMIT)
        act = h_gate * jax.nn.sigmoid(SWIGLU_ALPHA * h_gate) * (h_up + 1.0)
        y = jnp.dot(act.astype(BF16), wd_bf_ref[...], preferred_element_type=F32) + bd_ref[0]
        ys_ref[...] = y


def _experts(tile_e, n_used, ptiles, xs, wgu, bgu, wd, bd, tm):
    n_rows, w = xs.shape
    e, d, de2 = wgu.shape
    tile = lambda p, te, nu, pt: jnp.minimum(p, nu[0] - 1)
    expert = lambda p, te, nu, pt: te[tile(p, te, nu, pt)]
    return pl.pallas_call(
        _experts_body,
        out_shape=jax.ShapeDtypeStruct((n_rows, w), F32),
        grid_spec=pltpu.PrefetchScalarGridSpec(
            num_scalar_prefetch=3,
            grid=(n_rows // tm,),
            in_specs=[
                pl.BlockSpec((tm, w), lambda p, te, nu, pt: (tile(p, te, nu, pt), 0)),
                pl.BlockSpec(memory_space=pl.ANY),
                pl.BlockSpec((1, 1, de2), lambda p, te, nu, pt: (expert(p, te, nu, pt), 0, 0)),
                pl.BlockSpec(memory_space=pl.ANY),
                pl.BlockSpec((1, 1, d), lambda p, te, nu, pt: (expert(p, te, nu, pt), 0, 0)),
            ],
            out_specs=pl.BlockSpec((tm, w), lambda p, te, nu, pt: (tile(p, te, nu, pt), 0)),
            scratch_shapes=[pltpu.VMEM((d, de2), F32), pltpu.VMEM((de2 // 2, d), F32),
                            pltpu.VMEM((d, de2), BF16), pltpu.VMEM((de2 // 2, d), BF16),
                            pltpu.SemaphoreType.DMA((2,))],
        ),
        compiler_params=_params(1),
        name="moe_experts",
    )(tile_e, n_used, ptiles, xs, wgu, bgu, wd, bd)


def _combine_body(dest_ref, dest_next_ref, ys_hbm, h2_ref, gate_ref, nw_ref, o_ref, buf_ref, sems, *, th):
    i = pl.program_id(0)
    half_words = TOP_K * th

    def row_copy(src_row, slot, k, r):
        return pltpu.make_async_copy(ys_hbm.at[pl.ds(src_row, 1)], buf_ref.at[slot, k, pl.ds(r, 1)],
                                     sems.at[slot])

    def issue(idx_ref, base, slot):
        for r in range(th):
            for k in range(TOP_K):
                row_copy(idx_ref[base + k * th + r], slot, k, r).start(priority=k % DMA_PRIORITIES)

    def drain(slot):
        def body(r, carry):
            for k in range(TOP_K):
                row_copy(0, slot, k, 0).wait()
            return carry
        lax.fori_loop(0, th, body, 0, unroll=8)

    g = gate_ref[...]
    g_cols = jnp.concatenate([g, jnp.zeros((LANES - g.shape[0], 2 * th), F32)], axis=0).T

    def finish(half):
        rows = slice(half * th, (half + 1) * th)
        h = h2_ref[rows, :]
        for k in range(TOP_K):
            h = h + g_cols[rows, k:k + 1] * buf_ref[half, k]
        o_ref[rows, :] = h * lax.rsqrt(jnp.mean(h * h, axis=-1, keepdims=True) + EPS) * nw_ref[...]

    @pl.when(i == 0)
    def _first_half_tile():
        def body(r, carry):
            for k in range(TOP_K):
                row_copy(dest_ref[k * th + r], 0, k, r).start(priority=k % DMA_PRIORITIES)
            return carry
        lax.fori_loop(0, th, body, 0, unroll=8)

    issue(dest_ref, half_words, 1)
    drain(0)
    finish(0)

    @pl.when(i + 1 < pl.num_programs(0))
    def _next_step_first_half_tile():
        issue(dest_next_ref, 0, 0)

    drain(1)
    finish(1)


def _combine(dest_tiles, ys, h2, gates, norm_w, th):
    rows, d = h2.shape
    steps = rows // (2 * th)
    return pl.pallas_call(
        functools.partial(_combine_body, th=th),
        out_shape=jax.ShapeDtypeStruct((rows, d), F32),
        grid=(steps,),
        in_specs=[
            pl.BlockSpec((2 * TOP_K * th,), lambda i: (i,), memory_space=pltpu.SMEM),
            pl.BlockSpec((TOP_K * th,), lambda i: (jnp.minimum(2 * i + 2, 2 * steps - 1),),
                         memory_space=pltpu.SMEM),
            pl.BlockSpec(memory_space=pl.ANY),
            pl.BlockSpec((2 * th, d), lambda i: (i, 0)),
            pl.BlockSpec((8, 2 * th), lambda i: (0, i)),
            pl.BlockSpec((1, d), lambda i: (0, 0)),
        ],
        out_specs=pl.BlockSpec((2 * th, d), lambda i: (i, 0)),
        scratch_shapes=[pltpu.VMEM((2, TOP_K, th, d), F32), pltpu.SemaphoreType.DMA((2,))],
        compiler_params=_params(1),
        name="moe_combine",
    )(dest_tiles, dest_tiles, ys, h2, gates, norm_w)


def _rope_tables(positions):
    inv_freq = ROPE_BASE ** (-np.arange(0, RET_DK, 2, dtype=np.float64) / RET_DK)
    ang = np.asarray(positions, np.float64)[:, None] * inv_freq[None, :]
    return jnp.asarray(np.cos(ang), F32), jnp.asarray(np.sin(ang), F32)


def _retention_tables():
    c = RET_CHUNK
    log_gamma = jnp.log1p(-jnp.exp2(-5.0 - jnp.arange(RET_HEADS, dtype=F32)))
    idx = jnp.arange(c, dtype=F32)
    diff = idx[:, None] - idx[None, :]
    dmat = jnp.where(diff[None] >= 0, jnp.exp(log_gamma[:, None, None] * jnp.maximum(diff, 0.0)[None]), 0.0)
    zeta = jnp.exp(log_gamma[:, None] * (c - 1 - idx)[None, :])[:, :, None]
    xi = jnp.exp(log_gamma[:, None] * (idx + 1.0)[None, :])[:, :, None]
    chunk_decay = jnp.exp(log_gamma * c)
    return chunk_decay, dmat, zeta, xi


def _largest_tile(n, cap):
    t = cap
    while n % t:
        t //= 2
    return t


def _tile_major(a, tile):
    k, t = a.shape
    return a.reshape(k, t // tile, tile).transpose(1, 0, 2).reshape(-1)


def kernel(x, meta_tokens, norm_mix_w, w_in, conv_dw_w, conv_dw_b, conv_ln_w, conv_ln_b, w_conv_out,
           b_conv_out, ret_gn_w, w_ret_out, w_mix_out, norm_ffn_w, w_router, b_router, w_gate_up,
           b_gate_up, w_down, b_down, norm_final_w):
    batch, seq, d = x.shape
    depth = w_in.shape[0]
    n_experts = w_router.shape[-1]
    assert depth == 1 and d == 1024 and seq % RET_CHUNK == 0
    assert w_in.shape[-1] == 10 * 1024 and w_gate_up.shape[-1] == 2 * d
    rows = batch * seq

    tm_in = _largest_tile(seq, 512)
    tt_ret = _largest_tile(seq, 512)
    tt_tail = _largest_tile(seq, 512)
    td = _largest_tile(rows, 512)
    tc_half = _largest_tile(rows, 512) // 2
    tm_e = 1024

    x2d = x.reshape(rows, d)
    w_in_bf = w_in[0].astype(BF16)
    row1 = lambda v: v.reshape(1, -1)
    norm_w = row1(norm_mix_w[0])

    cos, sin = _rope_tables(N_META + np.arange(seq))
    u, qk, v, sg, gates = _inproj(x2d, norm_w, w_in_bf, cos, sin, tm_in)

    pad = RET_CHUNK - N_META
    meta_chunk = jnp.concatenate([jnp.zeros((pad, d), x.dtype), meta_tokens.astype(x.dtype)], axis=0)
    cos_m, sin_m = _rope_tables(np.maximum(np.arange(RET_CHUNK) - pad, 0))
    u_meta, qk_meta, v_meta, _, _ = _inproj(meta_chunk, norm_w, w_in_bf, cos_m, sin_m, RET_CHUNK)

    yg = _retention(qk, v, sg, qk_meta, v_meta, _retention_tables(), row1(ret_gn_w[0]), batch, seq, tt_ret)

    tail_params = dict(
        dw_w=conv_dw_w[0], dw_b=row1(conv_dw_b[0]), ln_w=row1(conv_ln_w[0]), ln_b=row1(conv_ln_b[0]),
        w_conv_out=w_conv_out[0].astype(BF16), b_conv_out=row1(b_conv_out[0]),
        w_ret_out=w_ret_out[0].astype(BF16), w_mix_out=w_mix_out[0].astype(BF16),
        norm_ffn_w=row1(norm_ffn_w[0]), w_router_t=w_router[0].T, b_router=b_router[0].reshape(-1, 1))
    h2, xn, topi, gate_w, rank, counts = _tail(x2d, u, gates, yg, u_meta, tail_params, batch, seq, tt_tail,
                                               n_experts)

    counts = counts[:, 0].astype(I32)
    pcounts = (counts + tm_e - 1) // tm_e * tm_e
    pends = jnp.cumsum(pcounts).astype(I32)
    pstarts = pends - pcounts
    experts = jnp.arange(n_experts, dtype=I32)
    start_of = jnp.sum(jnp.where(topi[:TOP_K, :, None] == experts, pstarts, 0), axis=-1)
    dest = start_of + rank[:TOP_K]
    n_tiles = rows * TOP_K // tm_e + n_experts
    tile_starts = jnp.arange(n_tiles, dtype=I32) * tm_e
    tile_e = jnp.minimum(jnp.sum(pends[None, :] <= tile_starts[:, None], axis=-1), n_experts - 1).astype(I32)
    n_used = (pends[-1:] // tm_e).astype(I32)

    xs = _dispatch(pends, pcounts, _tile_major(dest, td), xn, n_tiles * tm_e, td, tm_e, n_experts)
    ys = _experts(tile_e, n_used, pcounts // tm_e, xs, w_gate_up[0], b_gate_up[0][:, None, :],
                  w_down[0], b_down[0][:, None, :], tm_e)
    out = _combine(_tile_major(dest, tc_half), ys, h2, gate_w, row1(norm_final_w), tc_half)
    return out.reshape(batch, seq, d)
```

```python
import functools

import jax
import jax.numpy as jnp
import numpy as np
from jax import lax
from jax.experimental import pallas as pl
from jax.experimental.pallas import tpu as pltpu

N_META = 16
CONV_WIDTH = 31
RET_HEADS = 4
RET_DK = 256
RET_DV = 512
RET_CHUNK = 128
ROPE_BASE = 10000.0
TOP_K = 4
SWIGLU_LIMIT = 7.0
SWIGLU_ALPHA = 1.702
EPS = 1e-5

LANES = 128
SUBLANES = 8
CONV_HALO = 32
DMA_PRIORITIES = 2
VMEM_LIMIT = 56 * 1024 * 1024

F32 = jnp.float32
BF16 = jnp.bfloat16
I32 = jnp.int32

WB_CONV_A, WB_CONV_B, WB_Q, WB_V, WB_G, WB_GATE = 0, 1, 2, 4, 6, 8


def _params(n_grid_axes):
    return pltpu.CompilerParams(dimension_semantics=("arbitrary",) * n_grid_axes,
                                vmem_limit_bytes=VMEM_LIMIT)


def _silu(x):
    return x * jax.nn.sigmoid(x)


def _inproj_body(x_ref, nw_ref, w_ref, cos_ref, sin_ref, u_ref, qk_ref, v_ref, sg_ref, gate_ref):
    x = x_ref[...]
    ms = jnp.mean(x * x, axis=-1, keepdims=True)
    hn = (x * lax.rsqrt(ms + EPS) * nw_ref[...]).astype(BF16)

    def proj(block):
        return jnp.dot(hn, w_ref[:, block * 1024:(block + 1) * 1024], preferred_element_type=F32)

    u_ref[...] = (proj(WB_CONV_A) * jax.nn.sigmoid(proj(WB_CONV_B))).astype(BF16)

    half = RET_DK // 2
    for j, scale in enumerate((RET_DK ** -0.5, 1.0)):
        acc = proj(WB_Q + j)
        cos = cos_ref[...] * scale
        sin = sin_ref[...] * scale
        for h in range(RET_HEADS):
            lo = j * 1024 + h * RET_DK
            x1 = acc[:, h * RET_DK:h * RET_DK + half]
            x2 = acc[:, h * RET_DK + half:(h + 1) * RET_DK]
            qk_ref[:, lo:lo + half] = (x1 * cos - x2 * sin).astype(BF16)
            qk_ref[:, lo + half:lo + RET_DK] = (x2 * cos + x1 * sin).astype(BF16)

    for j in range(2):
        cols = slice(j * 1024, (j + 1) * 1024)
        v_ref[:, cols] = proj(WB_V + j).astype(BF16)
        sg_ref[:, cols] = _silu(proj(WB_G + j)).astype(BF16)
        gate_ref[:, cols] = jax.nn.sigmoid(proj(WB_GATE + j)).astype(BF16)


def _inproj(x2d, norm_w, w_in_bf, cos, sin, tm):
    rows, d = x2d.shape
    period_tiles = cos.shape[0] // tm
    wide = pl.BlockSpec((tm, 2048), lambda i: (i, 0))
    return pl.pallas_call(
        _inproj_body,
        out_shape=(jax.ShapeDtypeStruct((rows, 1024), BF16),) + (jax.ShapeDtypeStruct((rows, 2048), BF16),) * 4,
        grid=(rows // tm,),
        in_specs=[
            pl.BlockSpec((tm, d), lambda i: (i, 0)),
            pl.BlockSpec((1, d), lambda i: (0, 0)),
            pl.BlockSpec(w_in_bf.shape, lambda i: (0, 0), pipeline_mode=pl.Buffered(1)),
            pl.BlockSpec((tm, LANES), lambda i: (i % period_tiles, 0)),
            pl.BlockSpec((tm, LANES), lambda i: (i % period_tiles, 0)),
        ],
        out_specs=(pl.BlockSpec((tm, 1024), lambda i: (i, 0)), wide, wide, wide, wide),
        compiler_params=_params(1),
        name="inproj",
    )(x2d, norm_w, w_in_bf, cos, sin)


def _dot_t0(a, b):
    return lax.dot_general(a, b, (((0,), (0,)), ((), ())), preferred_element_type=F32)


def _dot_t1(a, b):
    return lax.dot_general(a, b, (((1,), (1,)), ((), ())), preferred_element_type=F32)


def _retention_body(cd_ref, q_ref, k_ref, v_ref, sg_ref, km_ref, vm_ref, dmat_ref, zeta_ref, xi_ref, gnw_ref,
                    o_ref, state_ref, *, n_chunks):
    t = pl.program_id(1)
    C, DK, DV = RET_CHUNK, RET_DK, RET_DV

    def state_update(prev, h, k_h, v_h):
        kz = (k_h.astype(F32) * zeta_ref[h]).astype(BF16)
        return prev * cd_ref[h] + _dot_t0(kz, v_h)

    @pl.when(t == 0)
    def _seed_from_meta_chunk():
        for h in range(RET_HEADS):
            state_ref[h] = state_update(jnp.zeros((DK, DV), F32), h,
                                        km_ref[:, h * DK:(h + 1) * DK], vm_ref[:, h * DV:(h + 1) * DV])

    def chunk(c, carry):
        rows = pl.ds(pl.multiple_of(c * C, C), C)
        for h in range(RET_HEADS):
            q_h = q_ref[rows, h * DK:(h + 1) * DK]
            k_h = k_ref[rows, h * DK:(h + 1) * DK]
            v_h = v_ref[rows, h * DV:(h + 1) * DV]
            scores = _dot_t1(q_h, k_h) * dmat_ref[h]
            inner = jnp.dot(scores.astype(BF16), v_h, preferred_element_type=F32)
            state = state_ref[h]
            cross = jnp.dot(q_h, state.astype(BF16), preferred_element_type=F32) * xi_ref[h]
            state_ref[h] = state_update(state, h, k_h, v_h)
            y = inner + cross
            mu = jnp.mean(y, axis=-1, keepdims=True)
            yc = y - mu
            yn = yc * lax.rsqrt(jnp.mean(yc * yc, axis=-1, keepdims=True) + EPS)
            gated = yn * gnw_ref[:, h * DV:(h + 1) * DV] * sg_ref[rows, h * DV:(h + 1) * DV].astype(F32)
            o_ref[rows, h * DV:(h + 1) * DV] = gated.astype(BF16)
        return carry

    lax.fori_loop(0, n_chunks, chunk, 0)


def _retention(qk, v, sg, qk_meta, v_meta, tables, gn_w, batch, seq, tt):
    chunk_decay, dmat, zeta, xi = tables
    tiles = seq // tt
    row = lambda b, t: b * tiles + t
    hv = RET_HEADS * RET_DV
    full = lambda shape: pl.BlockSpec(shape, lambda b, t: (0,) * len(shape))
    return pl.pallas_call(
        functools.partial(_retention_body, n_chunks=tt // RET_CHUNK),
        out_shape=jax.ShapeDtypeStruct((batch * seq, hv), BF16),
        grid=(batch, tiles),
        in_specs=[
            pl.BlockSpec(memory_space=pltpu.SMEM),
            pl.BlockSpec((tt, 1024), lambda b, t: (row(b, t), 0)),
            pl.BlockSpec((tt, 1024), lambda b, t: (row(b, t), 1)),
            pl.BlockSpec((tt, hv), lambda b, t: (row(b, t), 0)),
            pl.BlockSpec((tt, hv), lambda b, t: (row(b, t), 0)),
            pl.BlockSpec((RET_CHUNK, 1024), lambda b, t: (0, 1)),
            full((RET_CHUNK, hv)),
            full((RET_HEADS, RET_CHUNK, RET_CHUNK)),
            full((RET_HEADS, RET_CHUNK, 1)),
            full((RET_HEADS, RET_CHUNK, 1)),
            full((1, hv)),
        ],
        out_specs=pl.BlockSpec((tt, hv), lambda b, t: (row(b, t), 0)),
        scratch_shapes=[pltpu.VMEM((RET_HEADS, RET_DK, RET_DV), F32)],
        compiler_params=_params(2),
        name="retention",
    )(chunk_decay, qk, qk, v, sg, qk_meta, v_meta, dmat, zeta, xi, gn_w)


def _tail_body(x_ref, u_ref, gates_ref, yg_ref, um_ref, dww_ref, dwb_ref, lnw_ref, lnb_ref,
               wco_ref, bco_ref, wro_ref, wmo_ref, nfw_ref, wrt_ref, br_ref, tri_ref,
               h2_ref, xn_ref, topi_ref, gate_ref, rank_ref, cnt_ref,
               uext_ref, phase_ref, conv_ref, count_ref, *, tt, n_experts):
    b = pl.program_id(0)
    t = pl.program_id(1)
    C = RET_CHUNK
    d = x_ref.shape[1]

    @pl.when((b == 0) & (t == 0))
    def _zero_counts():
        count_ref[...] = jnp.zeros_like(count_ref)

    @pl.when(t == 0)
    def _seed_conv_history():
        uext_ref[0:CONV_HALO, :] = um_ref[C - CONV_HALO:C, :].astype(F32)

    uext_ref[CONV_HALO:CONV_HALO + tt, :] = u_ref[...].astype(F32)

    first = CONV_HALO - (CONV_WIDTH - 1)
    for s in range(d // LANES):
        cols = slice(s * LANES, (s + 1) * LANES)
        for phase in range(1, SUBLANES):
            phase_ref[phase - 1] = uext_ref[phase:phase + phase_ref.shape[1], cols]
        for c in range(tt // C):
            acc = jnp.zeros((C, LANES), F32)
            for k in range(CONV_WIDTH):
                a, phase = divmod(first + k, SUBLANES)
                r0 = c * C + a * SUBLANES
                window = uext_ref[r0:r0 + C, cols] if phase == 0 else phase_ref[phase - 1, r0:r0 + C, :]
                acc = acc + dww_ref[k:k + 1, cols] * window
            conv_ref[c * C:(c + 1) * C, cols] = acc + dwb_ref[:, cols]
    uext_ref[0:CONV_HALO, :] = uext_ref[tt:tt + CONV_HALO, :]

    cv = conv_ref[...]
    mu = jnp.mean(cv, axis=-1, keepdims=True)
    cc = cv - mu
    ln = cc * lax.rsqrt(jnp.mean(cc * cc, axis=-1, keepdims=True) + EPS) * lnw_ref[...] + lnb_ref[...]
    act = (ln * jax.nn.sigmoid(ln)).astype(BF16)
    y_conv = jnp.dot(act, wco_ref[...], preferred_element_type=F32) + bco_ref[...]
    y_ret = jnp.dot(yg_ref[...], wro_ref[...], preferred_element_type=F32)
    merged = (gates_ref[:, :d].astype(F32) * y_conv + gates_ref[:, d:].astype(F32) * y_ret).astype(BF16)
    h2 = x_ref[...] + jnp.dot(merged, wmo_ref[...], preferred_element_type=F32)
    h2_ref[...] = h2

    xn = h2 * lax.rsqrt(jnp.mean(h2 * h2, axis=-1, keepdims=True) + EPS) * nfw_ref[...]
    xn_ref[...] = xn

    logits = lax.dot_general(wrt_ref[...], xn, (((1,), (1,)), ((), ())),
                             precision=lax.Precision.HIGHEST, preferred_element_type=F32) + br_ref[...]
    eidx = lax.broadcasted_iota(I32, (n_experts, tt), 0)
    selected = jnp.zeros((n_experts, tt), F32)
    vals, idxs = [], []
    for _ in range(TOP_K):
        m = jnp.max(logits, axis=0, keepdims=True)
        idx = jnp.min(jnp.where(logits == m, eidx, n_experts), axis=0, keepdims=True)
        hit = eidx == idx
        vals.append(m)
        idxs.append(idx)
        selected = jnp.where(hit, 1.0, selected)
        logits = jnp.where(hit, -jnp.inf, logits)
    exps = [jnp.exp(v - vals[0]) for v in vals]
    denom = exps[0] + exps[1] + exps[2] + exps[3]

    before = jnp.dot(selected.astype(BF16), tri_ref[...], preferred_element_type=F32) + count_ref[:, 0:1]
    ranks = [jnp.sum(jnp.where(eidx == idx, before, 0.0), axis=0, keepdims=True).astype(I32) for idx in idxs]
    count_ref[...] = count_ref[...] + jnp.sum(selected, axis=1, keepdims=True)

    zi = jnp.zeros((8 - TOP_K, tt), I32)
    topi_ref[...] = jnp.concatenate(idxs + [zi], axis=0)
    rank_ref[...] = jnp.concatenate(ranks + [zi], axis=0)
    gate_ref[...] = jnp.concatenate([e / denom for e in exps] + [zi.astype(F32)], axis=0)
    cnt_ref[...] = count_ref[...]


def _tail(x2d, u, gates, yg, u_meta, p, batch, seq, tt, n_experts):
    d = x2d.shape[1]
    tiles = seq // tt
    rows = batch * seq
    row = lambda b, t: b * tiles + t
    full = lambda shape: pl.BlockSpec(shape, lambda b, t: (0,) * len(shape), pipeline_mode=pl.Buffered(1))
    tri = (jnp.arange(tt)[:, None] < jnp.arange(tt)[None, :]).astype(BF16)
    lane_rows = pl.BlockSpec((8, tt), lambda b, t: (0, row(b, t)))
    return pl.pallas_call(
        functools.partial(_tail_body, tt=tt, n_experts=n_experts),
        out_shape=(
            jax.ShapeDtypeStruct((rows, d), F32),
            jax.ShapeDtypeStruct((rows, d), F32),
            jax.ShapeDtypeStruct((8, rows), I32),
            jax.ShapeDtypeStruct((8, rows), F32),
            jax.ShapeDtypeStruct((8, rows), I32),
            jax.ShapeDtypeStruct((n_experts, LANES), F32),
        ),
        grid=(batch, tiles),
        in_specs=[
            pl.BlockSpec((tt, d), lambda b, t: (row(b, t), 0)),
            pl.BlockSpec((tt, 1024), lambda b, t: (row(b, t), 0)),
            pl.BlockSpec((tt, 2048), lambda b, t: (row(b, t), 0)),
            pl.BlockSpec((tt, yg.shape[1]), lambda b, t: (row(b, t), 0)),
            full((RET_CHUNK, 1024)),
            full((CONV_WIDTH, d)), full((1, d)), full((1, d)), full((1, d)),
            full((d, d)), full((1, d)),
            full((yg.shape[1], d)),
            full((d, d)),
            full((1, d)),
            full((n_experts, d)), full((n_experts, 1)),
            full((tt, tt)),
        ],
        out_specs=(
            pl.BlockSpec((tt, d), lambda b, t: (row(b, t), 0)),
            pl.BlockSpec((tt, d), lambda b, t: (row(b, t), 0)),
            lane_rows, lane_rows, lane_rows,
            pl.BlockSpec((n_experts, LANES), lambda b, t: (0, 0)),
        ),
        scratch_shapes=[
            pltpu.VMEM((CONV_HALO + tt, d), F32),
            pltpu.VMEM((SUBLANES - 1, CONV_HALO + tt - SUBLANES, LANES), F32),
            pltpu.VMEM((tt, d), F32),
            pltpu.VMEM((n_experts, LANES), F32),
        ],
        compiler_params=_params(2),
        name="mixer_tail",
    )(x2d, u, gates, yg, u_meta, p["dw_w"], p["dw_b"], p["ln_w"], p["ln_b"], p["w_conv_out"], p["b_conv_out"],
      p["w_ret_out"], p["w_mix_out"], p["norm_ffn_w"], p["w_router_t"], p["b_router"], tri)


def _dispatch_body(pend_ref, pcnt_ref, dest_ref, xn_ref, xs_hbm, zero_ref, sem, zsem, *, td, tm, n_experts):
    def zero_copy(e):
        start = pl.multiple_of(pend_ref[e] - tm, tm)
        return pltpu.make_async_copy(zero_ref, xs_hbm.at[pl.ds(start, tm)], zsem)

    @pl.when(pl.program_id(0) == 0)
    def _zero_last_tile_of_every_expert():
        zero_ref[...] = jnp.zeros_like(zero_ref)
        for e in range(n_experts):
            pl.when(pcnt_ref[e] > 0)(lambda e=e: zero_copy(e).start())
        for e in range(n_experts):
            pl.when(pcnt_ref[e] > 0)(lambda e=e: zero_copy(e).wait())

    def row_copy(r, dst_row):
        return pltpu.make_async_copy(xn_ref.at[pl.ds(r, 1)], xs_hbm.at[pl.ds(dst_row, 1)], sem)

    for r in range(td):
        for k in range(TOP_K):
            row_copy(r, dest_ref[k * td + r]).start(priority=k % DMA_PRIORITIES)

    def drain(r, carry):
        for k in range(TOP_K):
            row_copy(0, 0).wait()
        return carry

    lax.fori_loop(0, td, drain, 0, unroll=8)


def _dispatch(pends, pcounts, dest_tiles, xn, n_rows_out, td, tm, n_experts):
    rows, w = xn.shape
    return pl.pallas_call(
        functools.partial(_dispatch_body, td=td, tm=tm, n_experts=n_experts),
        out_shape=jax.ShapeDtypeStruct((n_rows_out, w), F32),
        grid_spec=pltpu.PrefetchScalarGridSpec(
            num_scalar_prefetch=2,
            grid=(rows // td,),
            in_specs=[
                pl.BlockSpec((TOP_K * td,), lambda i, *_: (i,), memory_space=pltpu.SMEM),
                pl.BlockSpec((td, w), lambda i, *_: (i, 0)),
            ],
            out_specs=pl.BlockSpec(memory_space=pl.ANY),
            scratch_shapes=[pltpu.VMEM((tm, w), F32), pltpu.SemaphoreType.DMA, pltpu.SemaphoreType.DMA],
        ),
        compiler_params=_params(1),
        name="moe_dispatch",
    )(pends, pcounts, dest_tiles, xn)


def _experts_body(tile_e_ref, n_used_ref, ptiles_ref, xs_ref, wgu_hbm, bgu_ref, wd_hbm, bd_ref, ys_ref,
                  wgu_stage_ref, wd_stage_ref, wgu_bf_ref, wd_bf_ref, sems):
    p = pl.program_id(0)
    n_used = n_used_ref[0]

    def weight_copies(e):
        return (pltpu.make_async_copy(wgu_hbm.at[e], wgu_stage_ref, sems.at[0]),
                pltpu.make_async_copy(wd_hbm.at[e], wd_stage_ref, sems.at[1]))

    @pl.when(p < n_used)
    def _():
        e = tile_e_ref[p]

        @pl.when(p == 0)
        def _fetch_first_expert():
            for c in weight_copies(e):
                c.start()

        @pl.when((p == 0) | (e != tile_e_ref[jnp.maximum(p - 1, 0)]))
        def _switch_expert():
            for c in weight_copies(e):
                c.wait()
            wgu_bf_ref[...] = wgu_stage_ref[...].astype(BF16)
            wd_bf_ref[...] = wd_stage_ref[...].astype(BF16)
            nxt = p + ptiles_ref[e]

            @pl.when(nxt < n_used)
            def _fetch_next_expert():
                for c in weight_copies(tile_e_ref[nxt]):
                    c.start()

        x = xs_ref[...].astype(BF16)
        hgu = jnp.dot(x, wgu_bf_ref[...], preferred_element_type=F32) + bgu_ref[0]
        de = hgu.shape[1] // 2
        h_gate = jnp.minimum(hgu[:, :de], SWIGLU_LIMIT)
        h_up = jnp.clip(hgu[:, de:], -SWIGLU_LIMIT, SWIGLU_LIMIT)
        act = h_gate * jax.nn.sigmoid(SWIGLU_ALPHA * h_gate) * (h_up + 1.0)
        y = jnp.dot(act.astype(BF16), wd_bf_ref[...], preferred_element_type=F32) + bd_ref[0]
        ys_ref[...] = y


def _experts(tile_e, n_used, ptiles, xs, wgu, bgu, wd, bd, tm):
    n_rows, w = xs.shape
    e, d, de2 = wgu.shape
    tile = lambda p, te, nu, pt: jnp.minimum(p, nu[0] - 1)
    expert = lambda p, te, nu, pt: te[tile(p, te, nu, pt)]
    return pl.pallas_call(
        _experts_body,
        out_shape=jax.ShapeDtypeStruct((n_rows, w), F32),
        grid_spec=pltpu.PrefetchScalarGridSpec(
            num_scalar_prefetch=3,
            grid=(n_rows // tm,),
            in_specs=[
                pl.BlockSpec((tm, w), lambda p, te, nu, pt: (tile(p, te, nu, pt), 0)),
                pl.BlockSpec(memory_space=pl.ANY),
                pl.BlockSpec((1, 1, de2), lambda p, te, nu, pt: (expert(p, te, nu, pt), 0, 0)),
                pl.BlockSpec(memory_space=pl.ANY),
                pl.BlockSpec((1, 1, d), lambda p, te, nu, pt: (expert(p, te, nu, pt), 0, 0)),
            ],
            out_specs=pl.BlockSpec((tm, w), lambda p, te, nu, pt: (tile(p, te, nu, pt), 0)),
            scratch_shapes=[pltpu.VMEM((d, de2), F32), pltpu.VMEM((de2 // 2, d), F32),
                            pltpu.VMEM((d, de2), BF16), pltpu.VMEM((de2 // 2, d), BF16),
                            pltpu.SemaphoreType.DMA((2,))],
        ),
        compiler_params=_params(1),
        name="moe_experts",
    )(tile_e, n_used, ptiles, xs, wgu, bgu, wd, bd)


def _combine_body(dest_ref, dest_next_ref, ys_hbm, h2_ref, gate_ref, nw_ref, o_ref, buf_ref, sems, *, th):
    i = pl.program_id(0)
    half_words = TOP_K * th

    def row_copy(src_row, slot, k, r):
        return pltpu.make_async_copy(ys_hbm.at[pl.ds(src_row, 1)], buf_ref.at[slot, k, pl.ds(r, 1)],
                                     sems.at[slot])

    def issue(idx_ref, base, slot):
        for r in range(th):
            for k in range(TOP_K):
                row_copy(idx_ref[base + k * th + r], slot, k, r).start(priority=k % DMA_PRIORITIES)

    def drain(slot):
        def body(r, carry):
            for k in range(TOP_K):
                row_copy(0, slot, k, 0).wait()
            return carry
        lax.fori_loop(0, th, body, 0, unroll=8)

    g = gate_ref[...]
    g_cols = jnp.concatenate([g, jnp.zeros((LANES - g.shape[0], 2 * th), F32)], axis=0).T

    def finish(half):
        rows = slice(half * th, (half + 1) * th)
        h = h2_ref[rows, :]
        for k in range(TOP_K):
            h = h + g_cols[rows, k:k + 1] * buf_ref[half, k]
        o_ref[rows, :] = h * lax.rsqrt(jnp.mean(h * h, axis=-1, keepdims=True) + EPS) * nw_ref[...]

    @pl.when(i == 0)
    def _first_half_tile():
        def body(r, carry):
            for k in range(TOP_K):
                row_copy(dest_ref[k * th + r], 0, k, r).start(priority=k % DMA_PRIORITIES)
            return carry
        lax.fori_loop(0, th, body, 0, unroll=8)

    issue(dest_ref, half_words, 1)
    drain(0)
    finish(0)

    @pl.when(i + 1 < pl.num_programs(0))
    def _next_step_first_half_tile():
        issue(dest_next_ref, 0, 0)

    drain(1)
    finish(1)


def _combine(dest_tiles, ys, h2, gates, norm_w, th):
    rows, d = h2.shape
    steps = rows // (2 * th)
    return pl.pallas_call(
        functools.partial(_combine_body, th=th),
        out_shape=jax.ShapeDtypeStruct((rows, d), F32),
        grid=(steps,),
        in_specs=[
            pl.BlockSpec((2 * TOP_K * th,), lambda i: (i,), memory_space=pltpu.SMEM),
            pl.BlockSpec((TOP_K * th,), lambda i: (jnp.minimum(2 * i + 2, 2 * steps - 1),),
                         memory_space=pltpu.SMEM),
            pl.BlockSpec(memory_space=pl.ANY),
            pl.BlockSpec((2 * th, d), lambda i: (i, 0)),
            pl.BlockSpec((8, 2 * th), lambda i: (0, i)),
            pl.BlockSpec((1, d), lambda i: (0, 0)),
        ],
        out_specs=pl.BlockSpec((2 * th, d), lambda i: (i, 0)),
        scratch_shapes=[pltpu.VMEM((2, TOP_K, th, d), F32), pltpu.SemaphoreType.DMA((2,))],
        compiler_params=_params(1),
        name="moe_combine",
    )(dest_tiles, dest_tiles, ys, h2, gates, norm_w)


def _rope_tables(positions):
    inv_freq = ROPE_BASE ** (-np.arange(0, RET_DK, 2, dtype=np.float64) / RET_DK)
    ang = np.asarray(positions, np.float64)[:, None] * inv_freq[None, :]
    return jnp.asarray(np.cos(ang), F32), jnp.asarray(np.sin(ang), F32)


def _retention_tables():
    c = RET_CHUNK
    log_gamma = jnp.log1p(-jnp.exp2(-5.0 - jnp.arange(RET_HEADS, dtype=F32)))
    idx = jnp.arange(c, dtype=F32)
    diff = idx[:, None] - idx[None, :]
    dmat = jnp.where(diff[None] >= 0, jnp.exp(log_gamma[:, None, None] * jnp.maximum(diff, 0.0)[None]), 0.0)
    zeta = jnp.exp(log_gamma[:, None] * (c - 1 - idx)[None, :])[:, :, None]
    xi = jnp.exp(log_gamma[:, None] * (idx + 1.0)[None, :])[:, :, None]
    chunk_decay = jnp.exp(log_gamma * c)
    return chunk_decay, dmat, zeta, xi


def _largest_tile(n, cap):
    t = cap
    while n % t:
        t //= 2
    return t


def _tile_major(a, tile):
    k, t = a.shape
    return a.reshape(k, t // tile, tile).transpose(1, 0, 2).reshape(-1)


def kernel(x, meta_tokens, norm_mix_w, w_in, conv_dw_w, conv_dw_b, conv_ln_w, conv_ln_b, w_conv_out,
           b_conv_out, ret_gn_w, w_ret_out, w_mix_out, norm_ffn_w, w_router, b_router, w_gate_up,
           b_gate_up, w_down, b_down, norm_final_w):
    batch, seq, d = x.shape
    depth = w_in.shape[0]
    n_experts = w_router.shape[-1]
    assert depth == 1 and d == 1024 and seq % RET_CHUNK == 0
    assert w_in.shape[-1] == 10 * 1024 and w_gate_up.shape[-1] == 2 * d
    rows = batch * seq

    tm_in = _largest_tile(seq, 512)
    tt_ret = _largest_tile(seq, 512)
    tt_tail = _largest_tile(seq, 512)
    td = _largest_tile(rows, 1024)
    tc_half = _largest_tile(rows, 512) // 2
    tm_e = 512

    x2d = x.reshape(rows, d)
    w_in_bf = w_in[0].astype(BF16)
    row1 = lambda v: v.reshape(1, -1)
    norm_w = row1(norm_mix_w[0])

    cos, sin = _rope_tables(N_META + np.arange(seq))
    u, qk, v, sg, gates = _inproj(x2d, norm_w, w_in_bf, cos, sin, tm_in)

    pad = RET_CHUNK - N_META
    meta_chunk = jnp.concatenate([jnp.zeros((pad, d), x.dtype), meta_tokens.astype(x.dtype)], axis=0)
    cos_m, sin_m = _rope_tables(np.maximum(np.arange(RET_CHUNK) - pad, 0))
    u_meta, qk_meta, v_meta, _, _ = _inproj(meta_chunk, norm_w, w_in_bf, cos_m, sin_m, RET_CHUNK)

    yg = _retention(qk, v, sg, qk_meta, v_meta, _retention_tables(), row1(ret_gn_w[0]), batch, seq, tt_ret)

    tail_params = dict(
        dw_w=conv_dw_w[0], dw_b=row1(conv_dw_b[0]), ln_w=row1(conv_ln_w[0]), ln_b=row1(conv_ln_b[0]),
        w_conv_out=w_conv_out[0].astype(BF16), b_conv_out=row1(b_conv_out[0]),
        w_ret_out=w_ret_out[0].astype(BF16), w_mix_out=w_mix_out[0].astype(BF16),
        norm_ffn_w=row1(norm_ffn_w[0]), w_router_t=w_router[0].T, b_router=b_router[0].reshape(-1, 1))
    h2, xn, topi, gate_w, rank, counts = _tail(x2d, u, gates, yg, u_meta, tail_params, batch, seq, tt_tail,
                                               n_experts)

    counts = counts[:, 0].astype(I32)
    pcounts = (counts + tm_e - 1) // tm_e * tm_e
    pends = jnp.cumsum(pcounts).astype(I32)
    pstarts = pends - pcounts
    experts = jnp.arange(n_experts, dtype=I32)
    start_of = jnp.sum(jnp.where(topi[:TOP_K, :, None] == experts, pstarts, 0), axis=-1)
    dest = start_of + rank[:TOP_K]
    n_tiles = rows * TOP_K // tm_e + n_experts
    tile_starts = jnp.arange(n_tiles, dtype=I32) * tm_e
    tile_e = jnp.minimum(jnp.sum(pends[None, :] <= tile_starts[:, None], axis=-1), n_experts - 1).astype(I32)
    n_used = (pends[-1:] // tm_e).astype(I32)

    xs = _dispatch(pends, pcounts, _tile_major(dest, td), xn, n_tiles * tm_e, td, tm_e, n_experts)
    ys = _experts(tile_e, n_used, pcounts // tm_e, xs, w_gate_up[0], b_gate_up[0][:, None, :],
                  w_down[0], b_down[0][:, None, :], tm_e)
    out = _combine(_tile_major(dest, tc_half), ys, h2, gate_w, row1(norm_final_w), tc_half)
    return out.reshape(batch, seq, d)
```

```python
import functools

import jax
import jax.numpy as jnp
import numpy as np
from jax import lax
from jax.experimental import pallas as pl
from jax.experimental.pallas import tpu as pltpu

N_META = 16
CONV_WIDTH = 31
RET_HEADS = 4
RET_DK = 256
RET_DV = 512
RET_CHUNK = 128
ROPE_BASE = 10000.0
TOP_K = 4
SWIGLU_LIMIT = 7.0
SWIGLU_ALPHA = 1.702
EPS = 1e-5

LANES = 128
SUBLANES = 8
CONV_HALO = 32
TAIL_GROUP = 2
DMA_PRIORITIES = 2
VMEM_LIMIT = 56 * 1024 * 1024

F32 = jnp.float32
BF16 = jnp.bfloat16
I32 = jnp.int32

WB_CONV_A, WB_CONV_B, WB_Q, WB_V, WB_G, WB_GATE = 0, 1, 2, 4, 6, 8


def _params(n_grid_axes):
    return pltpu.CompilerParams(dimension_semantics=("arbitrary",) * n_grid_axes,
                                vmem_limit_bytes=VMEM_LIMIT)


def _silu(x):
    return x * jax.nn.sigmoid(x)


def _inproj_body(x_ref, nw_ref, w_ref, cos_ref, sin_ref, u_ref, qk_ref, v_ref, sg_ref, gate_ref):
    x = x_ref[...]
    ms = jnp.mean(x * x, axis=-1, keepdims=True)
    hn = (x * lax.rsqrt(ms + EPS) * nw_ref[...]).astype(BF16)

    def proj(block):
        return jnp.dot(hn, w_ref[:, block * 1024:(block + 1) * 1024], preferred_element_type=F32)

    u_ref[...] = (proj(WB_CONV_A) * jax.nn.sigmoid(proj(WB_CONV_B))).astype(BF16)

    half = RET_DK // 2
    for j, scale in enumerate((RET_DK ** -0.5, 1.0)):
        acc = proj(WB_Q + j)
        cos = cos_ref[...] * scale
        sin = sin_ref[...] * scale
        for h in range(RET_HEADS):
            lo = j * 1024 + h * RET_DK
            x1 = acc[:, h * RET_DK:h * RET_DK + half]
            x2 = acc[:, h * RET_DK + half:(h + 1) * RET_DK]
            qk_ref[:, lo:lo + half] = (x1 * cos - x2 * sin).astype(BF16)
            qk_ref[:, lo + half:lo + RET_DK] = (x2 * cos + x1 * sin).astype(BF16)

    for j in range(2):
        cols = slice(j * 1024, (j + 1) * 1024)
        v_ref[:, cols] = proj(WB_V + j).astype(BF16)
        sg_ref[:, cols] = _silu(proj(WB_G + j)).astype(BF16)
        gate_ref[:, cols] = jax.nn.sigmoid(proj(WB_GATE + j)).astype(BF16)


def _inproj(x2d, norm_w, w_in_bf, cos, sin, tm):
    rows, d = x2d.shape
    period_tiles = cos.shape[0] // tm
    wide = pl.BlockSpec((tm, 2048), lambda i: (i, 0))
    return pl.pallas_call(
        _inproj_body,
        out_shape=(jax.ShapeDtypeStruct((rows, 1024), BF16),) + (jax.ShapeDtypeStruct((rows, 2048), BF16),) * 4,
        grid=(rows // tm,),
        in_specs=[
            pl.BlockSpec((tm, d), lambda i: (i, 0)),
            pl.BlockSpec((1, d), lambda i: (0, 0)),
            pl.BlockSpec(w_in_bf.shape, lambda i: (0, 0), pipeline_mode=pl.Buffered(1)),
            pl.BlockSpec((tm, LANES), lambda i: (i % period_tiles, 0)),
            pl.BlockSpec((tm, LANES), lambda i: (i % period_tiles, 0)),
        ],
        out_specs=(pl.BlockSpec((tm, 1024), lambda i: (i, 0)), wide, wide, wide, wide),
        compiler_params=_params(1),
        name="inproj",
    )(x2d, norm_w, w_in_bf, cos, sin)


def _dot_t0(a, b):
    return lax.dot_general(a, b, (((0,), (0,)), ((), ())), preferred_element_type=F32)


def _dot_t1(a, b):
    return lax.dot_general(a, b, (((1,), (1,)), ((), ())), preferred_element_type=F32)


def _retention_body(cd_ref, q_ref, k_ref, v_ref, sg_ref, km_ref, vm_ref, dmat_ref, zeta_ref, xi_ref, gnw_ref,
                    o_ref, state_ref, *, n_chunks):
    t = pl.program_id(1)
    C, DK, DV = RET_CHUNK, RET_DK, RET_DV

    def state_update(prev, h, k_h, v_h):
        kz = (k_h.astype(F32) * zeta_ref[h]).astype(BF16)
        return prev * cd_ref[h] + _dot_t0(kz, v_h)

    @pl.when(t == 0)
    def _seed_from_meta_chunk():
        for h in range(RET_HEADS):
            state_ref[h] = state_update(jnp.zeros((DK, DV), F32), h,
                                        km_ref[:, h * DK:(h + 1) * DK], vm_ref[:, h * DV:(h + 1) * DV])

    def chunk(c, carry):
        rows = pl.ds(pl.multiple_of(c * C, C), C)
        for h in range(RET_HEADS):
            q_h = q_ref[rows, h * DK:(h + 1) * DK]
            k_h = k_ref[rows, h * DK:(h + 1) * DK]
            v_h = v_ref[rows, h * DV:(h + 1) * DV]
            scores = _dot_t1(q_h, k_h) * dmat_ref[h]
            inner = jnp.dot(scores.astype(BF16), v_h, preferred_element_type=F32)
            state = state_ref[h]
            cross = jnp.dot(q_h, state.astype(BF16), preferred_element_type=F32) * xi_ref[h]
            state_ref[h] = state_update(state, h, k_h, v_h)
            y = inner + cross
            mu = jnp.mean(y, axis=-1, keepdims=True)
            yc = y - mu
            yn = yc * lax.rsqrt(jnp.mean(yc * yc, axis=-1, keepdims=True) + EPS)
            gated = yn * gnw_ref[:, h * DV:(h + 1) * DV] * sg_ref[rows, h * DV:(h + 1) * DV].astype(F32)
            o_ref[rows, h * DV:(h + 1) * DV] = gated.astype(BF16)
        return carry

    lax.fori_loop(0, n_chunks, chunk, 0)


def _retention(qk, v, sg, qk_meta, v_meta, tables, gn_w, batch, seq, tt):
    chunk_decay, dmat, zeta, xi = tables
    tiles = seq // tt
    row = lambda b, t: b * tiles + t
    hv = RET_HEADS * RET_DV
    full = lambda shape: pl.BlockSpec(shape, lambda b, t: (0,) * len(shape))
    return pl.pallas_call(
        functools.partial(_retention_body, n_chunks=tt // RET_CHUNK),
        out_shape=jax.ShapeDtypeStruct((batch * seq, hv), BF16),
        grid=(batch, tiles),
        in_specs=[
            pl.BlockSpec(memory_space=pltpu.SMEM),
            pl.BlockSpec((tt, 1024), lambda b, t: (row(b, t), 0)),
            pl.BlockSpec((tt, 1024), lambda b, t: (row(b, t), 1)),
            pl.BlockSpec((tt, hv), lambda b, t: (row(b, t), 0)),
            pl.BlockSpec((tt, hv), lambda b, t: (row(b, t), 0)),
            pl.BlockSpec((RET_CHUNK, 1024), lambda b, t: (0, 1)),
            full((RET_CHUNK, hv)),
            full((RET_HEADS, RET_CHUNK, RET_CHUNK)),
            full((RET_HEADS, RET_CHUNK, 1)),
            full((RET_HEADS, RET_CHUNK, 1)),
            full((1, hv)),
        ],
        out_specs=pl.BlockSpec((tt, hv), lambda b, t: (row(b, t), 0)),
        scratch_shapes=[pltpu.VMEM((RET_HEADS, RET_DK, RET_DV), F32)],
        compiler_params=_params(2),
        name="retention",
    )(chunk_decay, qk, qk, v, sg, qk_meta, v_meta, dmat, zeta, xi, gn_w)


def _tail_body(x_ref, u_ref, gates_ref, yg_ref, um_ref, dww_ref, dwb_ref, lnw_ref, lnb_ref,
               wco_ref, bco_ref, wro_ref, wmo_ref, nfw_ref, wrt_ref, br_ref, tri_ref,
               h2_ref, xn_ref, topi_ref, gate_ref, rank_ref, cnt_ref,
               uext_ref, phase_ref, count_ref, *, tt, n_experts):
    b = pl.program_id(0)
    t = pl.program_id(1)
    C = RET_CHUNK
    d = x_ref.shape[1]

    @pl.when((b == 0) & (t == 0))
    def _zero_counts():
        count_ref[...] = jnp.zeros_like(count_ref)

    @pl.when(t == 0)
    def _seed_conv_history():
        uext_ref[0:CONV_HALO, :] = um_ref[C - CONV_HALO:C, :].astype(F32)

    uext_ref[CONV_HALO:CONV_HALO + tt, :] = u_ref[...].astype(F32)

    first = CONV_HALO - (CONV_WIDTH - 1)
    for s in range(d // LANES):
        cols = slice(s * LANES, (s + 1) * LANES)
        for phase in range(1, SUBLANES):
            phase_ref[s, phase - 1] = uext_ref[phase:phase + phase_ref.shape[2], cols]

    for g in range(tt // (TAIL_GROUP * C)):
        row_blocks = []
        for c in range(g * TAIL_GROUP, (g + 1) * TAIL_GROUP):
            blocks = []
            for s in range(d // LANES):
                cols = slice(s * LANES, (s + 1) * LANES)
                acc = jnp.zeros((C, LANES), F32)
                for k in range(CONV_WIDTH):
                    a, phase = divmod(first + k, SUBLANES)
                    r0 = c * C + a * SUBLANES
                    window = uext_ref[r0:r0 + C, cols] if phase == 0 else phase_ref[s, phase - 1, r0:r0 + C, :]
                    acc = acc + dww_ref[k:k + 1, cols] * window
                blocks.append(acc + dwb_ref[:, cols])
            row_blocks.append(jnp.concatenate(blocks, axis=1))
        cv = jnp.concatenate(row_blocks, axis=0)
        rows = slice(g * TAIL_GROUP * C, (g + 1) * TAIL_GROUP * C)
        mu = jnp.mean(cv, axis=-1, keepdims=True)
        cc = cv - mu
        ln = cc * lax.rsqrt(jnp.mean(cc * cc, axis=-1, keepdims=True) + EPS) * lnw_ref[...] + lnb_ref[...]
        act = (ln * jax.nn.sigmoid(ln)).astype(BF16)
        y_conv = jnp.dot(act, wco_ref[...], preferred_element_type=F32) + bco_ref[...]
        y_ret = jnp.dot(yg_ref[rows, :], wro_ref[...], preferred_element_type=F32)
        merged = (gates_ref[rows, :d].astype(F32) * y_conv + gates_ref[rows, d:].astype(F32) * y_ret).astype(BF16)
        h2_ref[rows, :] = x_ref[rows, :] + jnp.dot(merged, wmo_ref[...], preferred_element_type=F32)
    uext_ref[0:CONV_HALO, :] = uext_ref[tt:tt + CONV_HALO, :]
    h2 = h2_ref[...]

    xn = h2 * lax.rsqrt(jnp.mean(h2 * h2, axis=-1, keepdims=True) + EPS) * nfw_ref[...]
    xn_ref[...] = xn

    logits = lax.dot_general(wrt_ref[...], xn, (((1,), (1,)), ((), ())),
                             precision=lax.Precision.HIGHEST, preferred_element_type=F32) + br_ref[...]
    eidx = lax.broadcasted_iota(I32, (n_experts, tt), 0)
    selected = jnp.zeros((n_experts, tt), F32)
    vals, idxs = [], []
    for _ in range(TOP_K):
        m = jnp.max(logits, axis=0, keepdims=True)
        idx = jnp.min(jnp.where(logits == m, eidx, n_experts), axis=0, keepdims=True)
        hit = eidx == idx
        vals.append(m)
        idxs.append(idx)
        selected = jnp.where(hit, 1.0, selected)
        logits = jnp.where(hit, -jnp.inf, logits)
    exps = [jnp.exp(v - vals[0]) for v in vals]
    denom = exps[0] + exps[1] + exps[2] + exps[3]

    before = jnp.dot(selected.astype(BF16), tri_ref[...], preferred_element_type=F32) + count_ref[:, 0:1]
    ranks = [jnp.sum(jnp.where(eidx == idx, before, 0.0), axis=0, keepdims=True).astype(I32) for idx in idxs]
    count_ref[...] = count_ref[...] + jnp.sum(selected, axis=1, keepdims=True)

    zi = jnp.zeros((8 - TOP_K, tt), I32)
    topi_ref[...] = jnp.concatenate(idxs + [zi], axis=0)
    rank_ref[...] = jnp.concatenate(ranks + [zi], axis=0)
    gate_ref[...] = jnp.concatenate([e / denom for e in exps] + [zi.astype(F32)], axis=0)
    cnt_ref[...] = count_ref[...]


def _tail(x2d, u, gates, yg, u_meta, p, batch, seq, tt, n_experts):
    d = x2d.shape[1]
    tiles = seq // tt
    rows = batch * seq
    row = lambda b, t: b * tiles + t
    full = lambda shape: pl.BlockSpec(shape, lambda b, t: (0,) * len(shape), pipeline_mode=pl.Buffered(1))
    tri = (jnp.arange(tt)[:, None] < jnp.arange(tt)[None, :]).astype(BF16)
    lane_rows = pl.BlockSpec((8, tt), lambda b, t: (0, row(b, t)))
    return pl.pallas_call(
        functools.partial(_tail_body, tt=tt, n_experts=n_experts),
        out_shape=(
            jax.ShapeDtypeStruct((rows, d), F32),
            jax.ShapeDtypeStruct((rows, d), F32),
            jax.ShapeDtypeStruct((8, rows), I32),
            jax.ShapeDtypeStruct((8, rows), F32),
            jax.ShapeDtypeStruct((8, rows), I32),
            jax.ShapeDtypeStruct((n_experts, LANES), F32),
        ),
        grid=(batch, tiles),
        in_specs=[
            pl.BlockSpec((tt, d), lambda b, t: (row(b, t), 0)),
            pl.BlockSpec((tt, 1024), lambda b, t: (row(b, t), 0)),
            pl.BlockSpec((tt, 2048), lambda b, t: (row(b, t), 0)),
            pl.BlockSpec((tt, yg.shape[1]), lambda b, t: (row(b, t), 0)),
            full((RET_CHUNK, 1024)),
            full((CONV_WIDTH, d)), full((1, d)), full((1, d)), full((1, d)),
            full((d, d)), full((1, d)),
            full((yg.shape[1], d)),
            full((d, d)),
            full((1, d)),
            full((n_experts, d)), full((n_experts, 1)),
            full((tt, tt)),
        ],
        out_specs=(
            pl.BlockSpec((tt, d), lambda b, t: (row(b, t), 0)),
            pl.BlockSpec((tt, d), lambda b, t: (row(b, t), 0)),
            lane_rows, lane_rows, lane_rows,
            pl.BlockSpec((n_experts, LANES), lambda b, t: (0, 0)),
        ),
        scratch_shapes=[
            pltpu.VMEM((CONV_HALO + tt, d), F32),
            pltpu.VMEM((d // LANES, SUBLANES - 1, CONV_HALO + tt - SUBLANES, LANES), F32),
            pltpu.VMEM((n_experts, LANES), F32),
        ],
        compiler_params=_params(2),
        name="mixer_tail",
    )(x2d, u, gates, yg, u_meta, p["dw_w"], p["dw_b"], p["ln_w"], p["ln_b"], p["w_conv_out"], p["b_conv_out"],
      p["w_ret_out"], p["w_mix_out"], p["norm_ffn_w"], p["w_router_t"], p["b_router"], tri)


def _dispatch_body(pend_ref, pcnt_ref, dest_ref, xn_ref, xs_hbm, zero_ref, sem, zsem, *, td, tm, n_experts):
    def zero_copy(e):
        start = pl.multiple_of(pend_ref[e] - tm, tm)
        return pltpu.make_async_copy(zero_ref, xs_hbm.at[pl.ds(start, tm)], zsem)

    @pl.when(pl.program_id(0) == 0)
    def _zero_last_tile_of_every_expert():
        zero_ref[...] = jnp.zeros_like(zero_ref)
        for e in range(n_experts):
            pl.when(pcnt_ref[e] > 0)(lambda e=e: zero_copy(e).start())
        for e in range(n_experts):
            pl.when(pcnt_ref[e] > 0)(lambda e=e: zero_copy(e).wait())

    def row_copy(r, dst_row):
        return pltpu.make_async_copy(xn_ref.at[pl.ds(r, 1)], xs_hbm.at[pl.ds(dst_row, 1)], sem)

    for r in range(td):
        for k in range(TOP_K):
            row_copy(r, dest_ref[k * td + r]).start(priority=k % DMA_PRIORITIES)

    def drain(r, carry):
        for k in range(TOP_K):
            row_copy(0, 0).wait()
        return carry

    lax.fori_loop(0, td, drain, 0, unroll=8)


def _dispatch(pends, pcounts, dest_tiles, xn, n_rows_out, td, tm, n_experts):
    rows, w = xn.shape
    return pl.pallas_call(
        functools.partial(_dispatch_body, td=td, tm=tm, n_experts=n_experts),
        out_shape=jax.ShapeDtypeStruct((n_rows_out, w), F32),
        grid_spec=pltpu.PrefetchScalarGridSpec(
            num_scalar_prefetch=2,
            grid=(rows // td,),
            in_specs=[
                pl.BlockSpec((TOP_K * td,), lambda i, *_: (i,), memory_space=pltpu.SMEM),
                pl.BlockSpec((td, w), lambda i, *_: (i, 0)),
            ],
            out_specs=pl.BlockSpec(memory_space=pl.ANY),
            scratch_shapes=[pltpu.VMEM((tm, w), F32), pltpu.SemaphoreType.DMA, pltpu.SemaphoreType.DMA],
        ),
        compiler_params=_params(1),
        name="moe_dispatch",
    )(pends, pcounts, dest_tiles, xn)


def _experts_body(tile_e_ref, n_used_ref, ptiles_ref, xs_ref, wgu_hbm, bgu_ref, wd_hbm, bd_ref, ys_ref,
                  wgu_stage_ref, wd_stage_ref, wgu_bf_ref, wd_bf_ref, sems):
    p = pl.program_id(0)
    n_used = n_used_ref[0]

    def weight_copies(e):
        return (pltpu.make_async_copy(wgu_hbm.at[e], wgu_stage_ref, sems.at[0]),
                pltpu.make_async_copy(wd_hbm.at[e], wd_stage_ref, sems.at[1]))

    @pl.when(p < n_used)
    def _():
        e = tile_e_ref[p]

        @pl.when(p == 0)
        def _fetch_first_expert():
            for c in weight_copies(e):
                c.start()

        @pl.when((p == 0) | (e != tile_e_ref[jnp.maximum(p - 1, 0)]))
        def _switch_expert():
            for c in weight_copies(e):
                c.wait()
            wgu_bf_ref[...] = wgu_stage_ref[...].astype(BF16)
            wd_bf_ref[...] = wd_stage_ref[...].astype(BF16)
            nxt = p + ptiles_ref[e]

            @pl.when(nxt < n_used)
            def _fetch_next_expert():
                for c in weight_copies(tile_e_ref[nxt]):
                    c.start()

        x = xs_ref[...].astype(BF16)
        hgu = jnp.dot(x, wgu_bf_ref[...], preferred_element_type=F32) + bgu_ref[0]
        de = hgu.shape[1] // 2
        h_gate = jnp.minimum(hgu[:, :de], SWIGLU_LIMIT)
        h_up = jnp.clip(hgu[:, de:], -SWIGLU_LIMIT, SWIGLU_LIMIT)
        act = h_gate * jax.nn.sigmoid(SWIGLU_ALPHA * h_gate) * (h_up + 1.0)
        y = jnp.dot(act.astype(BF16), wd_bf_ref[...], preferred_element_type=F32) + bd_ref[0]
        ys_ref[...] = y


def _experts(tile_e, n_used, ptiles, xs, wgu, bgu, wd, bd, tm):
    n_rows, w = xs.shape
    e, d, de2 = wgu.shape
    tile = lambda p, te, nu, pt: jnp.minimum(p, nu[0] - 1)
    expert = lambda p, te, nu, pt: te[tile(p, te, nu, pt)]
    return pl.pallas_call(
        _experts_body,
        out_shape=jax.ShapeDtypeStruct((n_rows, w), F32),
        grid_spec=pltpu.PrefetchScalarGridSpec(
            num_scalar_prefetch=3,
            grid=(n_rows // tm,),
            in_specs=[
                pl.BlockSpec((tm, w), lambda p, te, nu, pt: (tile(p, te, nu, pt), 0)),
                pl.BlockSpec(memory_space=pl.ANY),
                pl.BlockSpec((1, 1, de2), lambda p, te, nu, pt: (expert(p, te, nu, pt), 0, 0)),
                pl.BlockSpec(memory_space=pl.ANY),
                pl.BlockSpec((1, 1, d), lambda p, te, nu, pt: (expert(p, te, nu, pt), 0, 0)),
            ],
            out_specs=pl.BlockSpec((tm, w), lambda p, te, nu, pt: (tile(p, te, nu, pt), 0)),
            scratch_shapes=[pltpu.VMEM((d, de2), F32), pltpu.VMEM((de2 // 2, d), F32),
                            pltpu.VMEM((d, de2), BF16), pltpu.VMEM((de2 // 2, d), BF16),
                            pltpu.SemaphoreType.DMA((2,))],
        ),
        compiler_params=_params(1),
        name="moe_experts",
    )(tile_e, n_used, ptiles, xs, wgu, bgu, wd, bd)


def _combine_body(dest_ref, dest_next_ref, ys_hbm, h2_ref, gate_ref, nw_ref, o_ref, buf_ref, sems, *, th):
    i = pl.program_id(0)
    half_words = TOP_K * th

    def row_copy(src_row, slot, k, r):
        return pltpu.make_async_copy(ys_hbm.at[pl.ds(src_row, 1)], buf_ref.at[slot, k, pl.ds(r, 1)],
                                     sems.at[slot])

    def issue(idx_ref, base, slot):
        for r in range(th):
            for k in range(TOP_K):
                row_copy(idx_ref[base + k * th + r], slot, k, r).start(priority=k % DMA_PRIORITIES)

    def drain(slot):
        def body(r, carry):
            for k in range(TOP_K):
                row_copy(0, slot, k, 0).wait()
            return carry
        lax.fori_loop(0, th, body, 0, unroll=8)

    g = gate_ref[...]
    g_cols = jnp.concatenate([g, jnp.zeros((LANES - g.shape[0], 2 * th), F32)], axis=0).T

    def finish(half):
        rows = slice(half * th, (half + 1) * th)
        h = h2_ref[rows, :]
        for k in range(TOP_K):
            h = h + g_cols[rows, k:k + 1] * buf_ref[half, k]
        o_ref[rows, :] = h * lax.rsqrt(jnp.mean(h * h, axis=-1, keepdims=True) + EPS) * nw_ref[...]

    @pl.when(i == 0)
    def _first_half_tile():
        def body(r, carry):
            for k in range(TOP_K):
                row_copy(dest_ref[k * th + r], 0, k, r).start(priority=k % DMA_PRIORITIES)
            return carry
        lax.fori_loop(0, th, body, 0, unroll=8)

    issue(dest_ref, half_words, 1)
    drain(0)
    finish(0)

    @pl.when(i + 1 < pl.num_programs(0))
    def _next_step_first_half_tile():
        issue(dest_next_ref, 0, 0)

    drain(1)
    finish(1)


def _combine(dest_tiles, ys, h2, gates, norm_w, th):
    rows, d = h2.shape
    steps = rows // (2 * th)
    return pl.pallas_call(
        functools.partial(_combine_body, th=th),
        out_shape=jax.ShapeDtypeStruct((rows, d), F32),
        grid=(steps,),
        in_specs=[
            pl.BlockSpec((2 * TOP_K * th,), lambda i: (i,), memory_space=pltpu.SMEM),
            pl.BlockSpec((TOP_K * th,), lambda i: (jnp.minimum(2 * i + 2, 2 * steps - 1),),
                         memory_space=pltpu.SMEM),
            pl.BlockSpec(memory_space=pl.ANY),
            pl.BlockSpec((2 * th, d), lambda i: (i, 0)),
            pl.BlockSpec((8, 2 * th), lambda i: (0, i)),
            pl.BlockSpec((1, d), lambda i: (0, 0)),
        ],
        out_specs=pl.BlockSpec((2 * th, d), lambda i: (i, 0)),
        scratch_shapes=[pltpu.VMEM((2, TOP_K, th, d), F32), pltpu.SemaphoreType.DMA((2,))],
        compiler_params=_params(1),
        name="moe_combine",
    )(dest_tiles, dest_tiles, ys, h2, gates, norm_w)


def _rope_tables(positions):
    inv_freq = ROPE_BASE ** (-np.arange(0, RET_DK, 2, dtype=np.float64) / RET_DK)
    ang = np.asarray(positions, np.float64)[:, None] * inv_freq[None, :]
    return jnp.asarray(np.cos(ang), F32), jnp.asarray(np.sin(ang), F32)


def _retention_tables():
    c = RET_CHUNK
    log_gamma = jnp.log1p(-jnp.exp2(-5.0 - jnp.arange(RET_HEADS, dtype=F32)))
    idx = jnp.arange(c, dtype=F32)
    diff = idx[:, None] - idx[None, :]
    dmat = jnp.where(diff[None] >= 0, jnp.exp(log_gamma[:, None, None] * jnp.maximum(diff, 0.0)[None]), 0.0)
    zeta = jnp.exp(log_gamma[:, None] * (c - 1 - idx)[None, :])[:, :, None]
    xi = jnp.exp(log_gamma[:, None] * (idx + 1.0)[None, :])[:, :, None]
    chunk_decay = jnp.exp(log_gamma * c)
    return chunk_decay, dmat, zeta, xi


def _largest_tile(n, cap):
    t = cap
    while n % t:
        t //= 2
    return t


def _tile_major(a, tile):
    k, t = a.shape
    return a.reshape(k, t // tile, tile).transpose(1, 0, 2).reshape(-1)


def kernel(x, meta_tokens, norm_mix_w, w_in, conv_dw_w, conv_dw_b, conv_ln_w, conv_ln_b, w_conv_out,
           b_conv_out, ret_gn_w, w_ret_out, w_mix_out, norm_ffn_w, w_router, b_router, w_gate_up,
           b_gate_up, w_down, b_down, norm_final_w):
    batch, seq, d = x.shape
    depth = w_in.shape[0]
    n_experts = w_router.shape[-1]
    assert depth == 1 and d == 1024 and seq % RET_CHUNK == 0
    assert w_in.shape[-1] == 10 * 1024 and w_gate_up.shape[-1] == 2 * d
    rows = batch * seq

    tm_in = _largest_tile(seq, 512)
    tt_ret = _largest_tile(seq, 512)
    tt_tail = _largest_tile(seq, 512)
    td = _largest_tile(rows, 1024)
    tc_half = _largest_tile(rows, 512) // 2
    tm_e = 512

    x2d = x.reshape(rows, d)
    w_in_bf = w_in[0].astype(BF16)
    row1 = lambda v: v.reshape(1, -1)
    norm_w = row1(norm_mix_w[0])

    cos, sin = _rope_tables(N_META + np.arange(seq))
    u, qk, v, sg, gates = _inproj(x2d, norm_w, w_in_bf, cos, sin, tm_in)

    pad = RET_CHUNK - N_META
    meta_chunk = jnp.concatenate([jnp.zeros((pad, d), x.dtype), meta_tokens.astype(x.dtype)], axis=0)
    cos_m, sin_m = _rope_tables(np.maximum(np.arange(RET_CHUNK) - pad, 0))
    u_meta, qk_meta, v_meta, _, _ = _inproj(meta_chunk, norm_w, w_in_bf, cos_m, sin_m, RET_CHUNK)

    yg = _retention(qk, v, sg, qk_meta, v_meta, _retention_tables(), row1(ret_gn_w[0]), batch, seq, tt_ret)

    tail_params = dict(
        dw_w=conv_dw_w[0], dw_b=row1(conv_dw_b[0]), ln_w=row1(conv_ln_w[0]), ln_b=row1(conv_ln_b[0]),
        w_conv_out=w_conv_out[0].astype(BF16), b_conv_out=row1(b_conv_out[0]),
        w_ret_out=w_ret_out[0].astype(BF16), w_mix_out=w_mix_out[0].astype(BF16),
        norm_ffn_w=row1(norm_ffn_w[0]), w_router_t=w_router[0].T, b_router=b_router[0].reshape(-1, 1))
    h2, xn, topi, gate_w, rank, counts = _tail(x2d, u, gates, yg, u_meta, tail_params, batch, seq, tt_tail,
                                               n_experts)

    counts = counts[:, 0].astype(I32)
    pcounts = (counts + tm_e - 1) // tm_e * tm_e
    pends = jnp.cumsum(pcounts).astype(I32)
    pstarts = pends - pcounts
    experts = jnp.arange(n_experts, dtype=I32)
    start_of = jnp.sum(jnp.where(topi[:TOP_K, :, None] == experts, pstarts, 0), axis=-1)
    dest = start_of + rank[:TOP_K]
    n_tiles = rows * TOP_K // tm_e + n_experts
    tile_starts = jnp.arange(n_tiles, dtype=I32) * tm_e
    tile_e = jnp.minimum(jnp.sum(pends[None, :] <= tile_starts[:, None], axis=-1), n_experts - 1).astype(I32)
    n_used = (pends[-1:] // tm_e).astype(I32)

    xs = _dispatch(pends, pcounts, _tile_major(dest, td), xn, n_tiles * tm_e, td, tm_e, n_experts)
    ys = _experts(tile_e, n_used, pcounts // tm_e, xs, w_gate_up[0], b_gate_up[0][:, None, :],
                  w_down[0], b_down[0][:, None, :], tm_e)
    out = _combine(_tile_major(dest, tc_half), ys, h2, gate_w, row1(norm_final_w), tc_half)
    return out.reshape(batch, seq, d)
```
